```python
import jax, jax.numpy as jnp
from jax import lax
import numpy as np

D_MODEL = 2048
BATCH = 8
SEQ = 8192
DEPTH = 4

N_MIXERS = 2
POOL_WINDOWS = (2, 4, 8, 16)
N_POOL_GROUPS = 4
POOL_GROUP_DIM = D_MODEL // N_POOL_GROUPS
HEAD_DIM = 128
N_HEADS = D_MODEL // HEAD_DIM
Q_BLOCK = 128
D_FF = 5504
CONV_WIDTH = 3
LN_EPS = 1e-5
DEEPNORM_ALPHA = (2.0 * DEPTH) ** 0.25
DEEPNORM_BETA = (8.0 * DEPTH) ** -0.25
N_POOL_LAYERS = (DEPTH + 1) // 2
N_ATTN_LAYERS = DEPTH // 2

kernel_name = "hybrid_pool_stickbreak_convffn_deepnorm"


def layer_norm(x, g, b):
    xf = x.astype(jnp.float32)
    mu = jnp.mean(xf, axis=-1, keepdims=True)
    var = jnp.mean(jnp.square(xf - mu), axis=-1, keepdims=True)
    y = (xf - mu) * lax.rsqrt(var + LN_EPS)
    return (y * g.astype(jnp.float32) + b.astype(jnp.float32)).astype(x.dtype)


def pool_mixer(x, w_groups, scale):
    B, S, D = x.shape
    xf = x.astype(jnp.float32)
    cs = jnp.cumsum(xf, axis=1).reshape(B, S, N_POOL_GROUPS, POOL_GROUP_DIM)
    xg = xf.reshape(B, S, N_POOL_GROUPS, POOL_GROUP_DIM)
    t1 = jnp.arange(1, S + 1)
    outs = []
    for g, w in enumerate(POOL_WINDOWS):
        c = cs[:, :, g]
        shifted = jnp.pad(c, ((0, 0), (w, 0), (0, 0)))[:, :S]
        cnt = jnp.minimum(t1, w).astype(jnp.float32)[None, :, None]
        outs.append((c - shifted) / cnt - xg[:, :, g])
    pooled = jnp.stack(outs, axis=2).astype(x.dtype)
    y = jnp.einsum('bsgc,gcd->bsgd', pooled, w_groups).reshape(B, S, D)
    return y * scale


def stick_breaking_attention(x, w_qkv, w_o):
    B, S, D = x.shape
    qkv = x @ w_qkv
    q, k, v = jnp.split(qkv, 3, axis=-1)
    nb = S // Q_BLOCK
    q_blocks = q.reshape(B, nb, Q_BLOCK, N_HEADS, HEAD_DIM).transpose(1, 0, 3, 2, 4)
    k = k.reshape(B, S, N_HEADS, HEAD_DIM).transpose(0, 2, 1, 3)
    v = v.reshape(B, S, N_HEADS, HEAD_DIM).transpose(0, 2, 1, 3)
    k_pos = jnp.arange(S)
    scale = HEAD_DIM ** -0.5

    def block(args):
        qb, bi = args
        z = jnp.einsum('bhqd,bhkd->bhqk', qb, k).astype(jnp.float32) * scale
        q_pos = bi * Q_BLOCK + jnp.arange(Q_BLOCK)
        mask = k_pos[None, :] < q_pos[:, None]
        log_not = jnp.where(mask, jax.nn.log_sigmoid(-z), 0.0)
        rest = lax.cumsum(log_not, axis=3, reverse=True)
        log_a = jnp.where(mask, z + rest, -jnp.inf)
        a = jnp.exp(log_a)
        return jnp.einsum('bhqk,bhkd->bhqd', a.astype(v.dtype), v)

    o = lax.map(block, (q_blocks, jnp.arange(nb)))
    o = o.transpose(1, 0, 3, 2, 4).reshape(B, S, D)
    return o @ w_o


def conv_ffn(x, w_up, conv_w, conv_b, w_down):
    S = x.shape[1]
    h = x @ w_up
    hp = jnp.pad(h, ((0, 0), (CONV_WIDTH - 1, 0), (0, 0)))
    hc = conv_b + conv_w[0] * hp[:, 0:S]
    for kk in range(1, CONV_WIDTH):
        hc = hc + conv_w[kk] * hp[:, kk:kk + S]
    gate, val = jnp.split(hc, 2, axis=-1)
    return (jax.nn.silu(gate) * val) @ w_down


def _fwd_setup_inputs(seed: int = 0) -> dict:
    key = jax.random.key(seed)
    ks = jax.random.split(key, 14)
    f32 = jnp.float32
    nrm = lambda k, shp: jax.random.normal(k, shp, dtype=f32)
    x = nrm(ks[0], (BATCH, SEQ, D_MODEL))
    pool_w = nrm(ks[1], (N_POOL_LAYERS, N_POOL_GROUPS, POOL_GROUP_DIM, POOL_GROUP_DIM)) * (POOL_GROUP_DIM ** -0.5) * DEEPNORM_BETA
    pool_scale = 1.0 + 0.02 * nrm(ks[2], (N_POOL_LAYERS, D_MODEL))
    attn_w_qkv = nrm(ks[3], (N_ATTN_LAYERS, D_MODEL, 3 * D_MODEL)) * (D_MODEL ** -0.5)
    attn_w_o = nrm(ks[4], (N_ATTN_LAYERS, D_MODEL, D_MODEL)) * (D_MODEL ** -0.5) * DEEPNORM_BETA
    ffn_w_up = nrm(ks[5], (DEPTH, D_MODEL, 2 * D_FF)) * (D_MODEL ** -0.5)
    ffn_conv_w = nrm(ks[6], (DEPTH, CONV_WIDTH, 2 * D_FF)) * (CONV_WIDTH ** -0.5)
    ffn_conv_b = 0.02 * nrm(ks[7], (DEPTH, 2 * D_FF))
    ffn_w_down = nrm(ks[8], (DEPTH, D_FF, D_MODEL)) * (D_FF ** -0.5) * DEEPNORM_BETA
    ln_mix_g = 1.0 + 0.02 * nrm(ks[9], (DEPTH, D_MODEL))
    ln_mix_b = 0.02 * nrm(ks[10], (DEPTH, D_MODEL))
    ln_ffn_g = 1.0 + 0.02 * nrm(ks[11], (DEPTH, D_MODEL))
    ln_ffn_b = 0.02 * nrm(ks[12], (DEPTH, D_MODEL))
    return {"x": x, "pool_w": pool_w, "pool_scale": pool_scale,
            "attn_w_qkv": attn_w_qkv, "attn_w_o": attn_w_o,
            "ffn_w_up": ffn_w_up, "ffn_conv_w": ffn_conv_w, "ffn_conv_b": ffn_conv_b,
            "ffn_w_down": ffn_w_down, "ln_mix_g": ln_mix_g, "ln_mix_b": ln_mix_b,
            "ln_ffn_g": ln_ffn_g, "ln_ffn_b": ln_ffn_b}


def _fwd_reference(x, pool_w, pool_scale, attn_w_qkv, attn_w_o, ffn_w_up, ffn_conv_w,
              ffn_conv_b, ffn_w_down, ln_mix_g, ln_mix_b, ln_ffn_g, ln_ffn_b):
    for i in range(DEPTH):
        j = i // N_MIXERS
        if i % N_MIXERS == 0:
            y = pool_mixer(x, pool_w[j], pool_scale[j])
        else:
            y = stick_breaking_attention(x, attn_w_qkv[j], attn_w_o[j])
        x = layer_norm(DEEPNORM_ALPHA * x + y, ln_mix_g[i], ln_mix_b[i])
        f = conv_ffn(x, ffn_w_up[i], ffn_conv_w[i], ffn_conv_b[i], ffn_w_down[i])
        x = layer_norm(DEEPNORM_ALPHA * x + f, ln_ffn_g[i], ln_ffn_b[i])
    return x


import jax as _jax
import jax.numpy as _jnp

TWIN_FORMAT = 'train_step'
FWD_PARAMS = ['x', 'pool_w', 'pool_scale', 'attn_w_qkv', 'attn_w_o', 'ffn_w_up', 'ffn_conv_w', 'ffn_conv_b', 'ffn_w_down', 'ln_mix_g', 'ln_mix_b', 'ln_ffn_g', 'ln_ffn_b']
TWIN_WEIGHTS = ['pool_w', 'pool_scale', 'attn_w_qkv', 'attn_w_o', 'ffn_w_up', 'ffn_conv_w', 'ffn_conv_b', 'ffn_w_down', 'ln_mix_g', 'ln_mix_b', 'ln_ffn_g', 'ln_ffn_b']
TWIN_DIFF_INPUT = 'x'
TWIN_INPUTS = ['x', 'pool_w', 'pool_scale', 'attn_w_qkv', 'attn_w_o', 'ffn_w_up', 'ffn_conv_w', 'ffn_conv_b', 'ffn_w_down', 'ln_mix_g', 'ln_mix_b', 'ln_ffn_g', 'ln_ffn_b', 'loss_target', 'm_pool_w', 'm_pool_scale', 'm_attn_w_qkv', 'm_attn_w_o', 'm_ffn_w_up', 'm_ffn_conv_w', 'm_ffn_conv_b', 'm_ffn_w_down', 'm_ln_mix_g', 'm_ln_mix_b', 'm_ln_ffn_g', 'm_ln_ffn_b', 'v_pool_w', 'v_pool_scale', 'v_attn_w_qkv', 'v_attn_w_o', 'v_ffn_w_up', 'v_ffn_conv_w', 'v_ffn_conv_b', 'v_ffn_w_down', 'v_ln_mix_g', 'v_ln_mix_b', 'v_ln_ffn_g', 'v_ln_ffn_b']
TWIN_OUTPUTS = ['loss', 'grad_x', 'grad_pool_w', 'grad_pool_scale', 'grad_attn_w_qkv', 'grad_attn_w_o', 'grad_ffn_w_up', 'grad_ffn_conv_w', 'grad_ffn_conv_b', 'grad_ffn_w_down', 'grad_ln_mix_g', 'grad_ln_mix_b', 'grad_ln_ffn_g', 'grad_ln_ffn_b', 'delta_pool_w', 'delta_pool_scale', 'delta_attn_w_qkv', 'delta_attn_w_o', 'delta_ffn_w_up', 'delta_ffn_conv_w', 'delta_ffn_conv_b', 'delta_ffn_w_down', 'delta_ln_mix_g', 'delta_ln_mix_b', 'delta_ln_ffn_g', 'delta_ln_ffn_b', 'new_m_pool_w', 'new_m_pool_scale', 'new_m_attn_w_qkv', 'new_m_attn_w_o', 'new_m_ffn_w_up', 'new_m_ffn_conv_w', 'new_m_ffn_conv_b', 'new_m_ffn_w_down', 'new_m_ln_mix_g', 'new_m_ln_mix_b', 'new_m_ln_ffn_g', 'new_m_ln_ffn_b', 'new_v_pool_w', 'new_v_pool_scale', 'new_v_attn_w_qkv', 'new_v_attn_w_o', 'new_v_ffn_w_up', 'new_v_ffn_conv_w', 'new_v_ffn_conv_b', 'new_v_ffn_w_down', 'new_v_ln_mix_g', 'new_v_ln_mix_b', 'new_v_ln_ffn_g', 'new_v_ln_ffn_b']
TWIN_LEAF_KINDS = {'loss': 'loss', 'grad_x': 'grad_x', 'grad_pool_w': 'grad_w', 'grad_pool_scale': 'grad_w', 'grad_attn_w_qkv': 'grad_w', 'grad_attn_w_o': 'grad_w', 'grad_ffn_w_up': 'grad_w', 'grad_ffn_conv_w': 'grad_w', 'grad_ffn_conv_b': 'grad_w', 'grad_ffn_w_down': 'grad_w', 'grad_ln_mix_g': 'grad_w', 'grad_ln_mix_b': 'grad_w', 'grad_ln_ffn_g': 'grad_w', 'grad_ln_ffn_b': 'grad_w', 'delta_pool_w': 'delta_w', 'delta_pool_scale': 'delta_w', 'delta_attn_w_qkv': 'delta_w', 'delta_attn_w_o': 'delta_w', 'delta_ffn_w_up': 'delta_w', 'delta_ffn_conv_w': 'delta_w', 'delta_ffn_conv_b': 'delta_w', 'delta_ffn_w_down': 'delta_w', 'delta_ln_mix_g': 'delta_w', 'delta_ln_mix_b': 'delta_w', 'delta_ln_ffn_g': 'delta_w', 'delta_ln_ffn_b': 'delta_w', 'new_m_pool_w': 'new_m', 'new_m_pool_scale': 'new_m', 'new_m_attn_w_qkv': 'new_m', 'new_m_attn_w_o': 'new_m', 'new_m_ffn_w_up': 'new_m', 'new_m_ffn_conv_w': 'new_m', 'new_m_ffn_conv_b': 'new_m', 'new_m_ffn_w_down': 'new_m', 'new_m_ln_mix_g': 'new_m', 'new_m_ln_mix_b': 'new_m', 'new_m_ln_ffn_g': 'new_m', 'new_m_ln_ffn_b': 'new_m', 'new_v_pool_w': 'new_v', 'new_v_pool_scale': 'new_v', 'new_v_attn_w_qkv': 'new_v', 'new_v_attn_w_o': 'new_v', 'new_v_ffn_w_up': 'new_v', 'new_v_ffn_conv_w': 'new_v', 'new_v_ffn_conv_b': 'new_v', 'new_v_ffn_w_down': 'new_v', 'new_v_ln_mix_g': 'new_v', 'new_v_ln_mix_b': 'new_v', 'new_v_ln_ffn_g': 'new_v', 'new_v_ln_ffn_b': 'new_v'}


def _forward(args):
    return _fwd_reference(*[args[k] for k in FWD_PARAMS])


def _output_shape():
    def fwd():
        inp = _fwd_setup_inputs(0)
        return _fwd_reference(*[inp[k] for k in FWD_PARAMS])
    out = _jax.eval_shape(fwd)
    return out.shape, out.dtype

N_MICROBATCH = 1
ADAM_LR = 0.001
ADAM_B1 = 0.9
ADAM_B2 = 0.999
ADAM_EPS = 1e-08
ADAM_WD = 0.01
ADAM_STEP = 10
PER_EXAMPLE_BATCH_AXIS = {'x': 0, 'loss_target': 0}
SHARED_INPUTS = []
_WEIGHT_DTYPES = {'pool_w': _jnp.float32, 'pool_scale': _jnp.float32, 'attn_w_qkv': _jnp.float32, 'attn_w_o': _jnp.float32, 'ffn_w_up': _jnp.float32, 'ffn_conv_w': _jnp.float32, 'ffn_conv_b': _jnp.float32, 'ffn_w_down': _jnp.float32, 'ln_mix_g': _jnp.float32, 'ln_mix_b': _jnp.float32, 'ln_ffn_g': _jnp.float32, 'ln_ffn_b': _jnp.float32}
MOMENT_SCALE = {'pool_w': 6.793331e-02, 'pool_scale': 4.220827e-02, 'attn_w_qkv': 1.426714e-02, 'attn_w_o': 4.932936e-02, 'ffn_w_up': 1.203648e-02, 'ffn_conv_w': 1.206033e-02, 'ffn_conv_b': 1.462735e-02, 'ffn_w_down': 4.628782e-02, 'ln_mix_g': 9.978434e-01, 'ln_mix_b': 5.158579e-01, 'ln_ffn_g': 1.607209e+01, 'ln_ffn_b': 8.980315e-01}


def _to_microbatches(a, axis):
    t = _jnp.moveaxis(a, axis, 0)
    t = t.reshape((N_MICROBATCH, t.shape[0] // N_MICROBATCH) + t.shape[1:])
    return _jnp.moveaxis(t, 1, axis + 1)


def setup_inputs(seed: int = 0) -> dict:
    inp = _fwd_setup_inputs(seed)
    key = _jax.random.fold_in(_jax.random.key(seed), 7919)
    shape, _ = _output_shape()
    out = dict(inp)
    out["loss_target"] = _jax.random.normal(_jax.random.fold_in(key, 0), shape, _jnp.float32)
    for i, name in enumerate(TWIN_WEIGHTS):
        w = inp[name].astype(_jnp.float32)
        if MOMENT_SCALE is None:
            s = _jnp.sqrt(_jnp.mean(_jnp.square(w)) + 1e-30)
        else:
            s = MOMENT_SCALE[name]
        km, kv = _jax.random.split(_jax.random.fold_in(key, i + 1))
        out[name] = w
        out["m_" + name] = s * _jax.random.normal(km, w.shape, _jnp.float32)
        out["v_" + name] = (s * s) * _jax.random.uniform(kv, w.shape, _jnp.float32, 0.5, 1.5)
    if N_MICROBATCH > 1:
        for name, axis in PER_EXAMPLE_BATCH_AXIS.items():
            out[name] = _to_microbatches(out[name], axis)
    return {'x': out['x'], 'pool_w': out['pool_w'], 'pool_scale': out['pool_scale'], 'attn_w_qkv': out['attn_w_qkv'], 'attn_w_o': out['attn_w_o'], 'ffn_w_up': out['ffn_w_up'], 'ffn_conv_w': out['ffn_conv_w'], 'ffn_conv_b': out['ffn_conv_b'], 'ffn_w_down': out['ffn_w_down'], 'ln_mix_g': out['ln_mix_g'], 'ln_mix_b': out['ln_mix_b'], 'ln_ffn_g': out['ln_ffn_g'], 'ln_ffn_b': out['ln_ffn_b'], 'loss_target': out['loss_target'], 'm_pool_w': out['m_pool_w'], 'm_pool_scale': out['m_pool_scale'], 'm_attn_w_qkv': out['m_attn_w_qkv'], 'm_attn_w_o': out['m_attn_w_o'], 'm_ffn_w_up': out['m_ffn_w_up'], 'm_ffn_conv_w': out['m_ffn_conv_w'], 'm_ffn_conv_b': out['m_ffn_conv_b'], 'm_ffn_w_down': out['m_ffn_w_down'], 'm_ln_mix_g': out['m_ln_mix_g'], 'm_ln_mix_b': out['m_ln_mix_b'], 'm_ln_ffn_g': out['m_ln_ffn_g'], 'm_ln_ffn_b': out['m_ln_ffn_b'], 'v_pool_w': out['v_pool_w'], 'v_pool_scale': out['v_pool_scale'], 'v_attn_w_qkv': out['v_attn_w_qkv'], 'v_attn_w_o': out['v_attn_w_o'], 'v_ffn_w_up': out['v_ffn_w_up'], 'v_ffn_conv_w': out['v_ffn_conv_w'], 'v_ffn_conv_b': out['v_ffn_conv_b'], 'v_ffn_w_down': out['v_ffn_w_down'], 'v_ln_mix_g': out['v_ln_mix_g'], 'v_ln_mix_b': out['v_ln_mix_b'], 'v_ln_ffn_g': out['v_ln_ffn_g'], 'v_ln_ffn_b': out['v_ln_ffn_b']}


def _loss(weights, diff, rest, loss_target):
    with _jax.named_scope("forward"):
        args = {**rest, TWIN_DIFF_INPUT: diff, **{k: w.astype(_WEIGHT_DTYPES[k]) for k, w in weights.items()}}
        y = _forward(args)
    with _jax.named_scope("loss_head"):
        err = _jnp.square(y.astype(_jnp.float32) - loss_target)
        return 0.5 * _jnp.sum(_jnp.mean(err, axis=-1)) if err.ndim else 0.5 * err


def _adamw(w, g, m, v):
    m = ADAM_B1 * m + (1.0 - ADAM_B1) * g
    v = ADAM_B2 * v + (1.0 - ADAM_B2) * _jnp.square(g)
    m_hat = m / (1.0 - ADAM_B1 ** ADAM_STEP)
    v_hat = v / (1.0 - ADAM_B2 ** ADAM_STEP)
    delta = -ADAM_LR * (m_hat / (_jnp.sqrt(v_hat) + ADAM_EPS) + ADAM_WD * w)
    return delta, m, v


def reference(x, pool_w, pool_scale, attn_w_qkv, attn_w_o, ffn_w_up, ffn_conv_w, ffn_conv_b, ffn_w_down, ln_mix_g, ln_mix_b, ln_ffn_g, ln_ffn_b, loss_target, m_pool_w, m_pool_scale, m_attn_w_qkv, m_attn_w_o, m_ffn_w_up, m_ffn_conv_w, m_ffn_conv_b, m_ffn_w_down, m_ln_mix_g, m_ln_mix_b, m_ln_ffn_g, m_ln_ffn_b, v_pool_w, v_pool_scale, v_attn_w_qkv, v_attn_w_o, v_ffn_w_up, v_ffn_conv_w, v_ffn_conv_b, v_ffn_w_down, v_ln_mix_g, v_ln_mix_b, v_ln_ffn_g, v_ln_ffn_b):
    given = dict(x=x, pool_w=pool_w, pool_scale=pool_scale, attn_w_qkv=attn_w_qkv, attn_w_o=attn_w_o, ffn_w_up=ffn_w_up, ffn_conv_w=ffn_conv_w, ffn_conv_b=ffn_conv_b, ffn_w_down=ffn_w_down, ln_mix_g=ln_mix_g, ln_mix_b=ln_mix_b, ln_ffn_g=ln_ffn_g, ln_ffn_b=ln_ffn_b, loss_target=loss_target, m_pool_w=m_pool_w, m_pool_scale=m_pool_scale, m_attn_w_qkv=m_attn_w_qkv, m_attn_w_o=m_attn_w_o, m_ffn_w_up=m_ffn_w_up, m_ffn_conv_w=m_ffn_conv_w, m_ffn_conv_b=m_ffn_conv_b, m_ffn_w_down=m_ffn_w_down, m_ln_mix_g=m_ln_mix_g, m_ln_mix_b=m_ln_mix_b, m_ln_ffn_g=m_ln_ffn_g, m_ln_ffn_b=m_ln_ffn_b, v_pool_w=v_pool_w, v_pool_scale=v_pool_scale, v_attn_w_qkv=v_attn_w_qkv, v_attn_w_o=v_attn_w_o, v_ffn_w_up=v_ffn_w_up, v_ffn_conv_w=v_ffn_conv_w, v_ffn_conv_b=v_ffn_conv_b, v_ffn_w_down=v_ffn_w_down, v_ln_mix_g=v_ln_mix_g, v_ln_mix_b=v_ln_mix_b, v_ln_ffn_g=v_ln_ffn_g, v_ln_ffn_b=v_ln_ffn_b)
    weights = {n: given[n] for n in TWIN_WEIGHTS}
    shared = {n: given[n] for n in SHARED_INPUTS}
    per_example = {n: given[n] for n in ['x']}
    grad_fn = _jax.value_and_grad(_loss, argnums=(0, 1))

    def one_microbatch(ex, loss_target):
        ex = dict(ex)
        diff = ex.pop(TWIN_DIFF_INPUT)
        return grad_fn(weights, diff, {**shared, **ex}, loss_target)

    if N_MICROBATCH == 1:
        loss, (grad_w, grad_x) = one_microbatch(per_example, given["loss_target"])
    else:
        def body(carry, xs):
            loss_sum, grad_sum = carry
            l_k, (gw_k, gx_k) = one_microbatch(xs[0], xs[1])
            with _jax.named_scope("update"):
                return (loss_sum + l_k, _jax.tree.map(_jnp.add, grad_sum, gw_k)), gx_k

        init = (_jnp.zeros((), _jnp.float32), _jax.tree.map(_jnp.zeros_like, weights))
        (loss, grad_w), grad_x = _jax.lax.scan(body, init, (per_example, given["loss_target"]))
    with _jax.named_scope("update"):
        delta_w, new_m, new_v = {}, {}, {}
        for n in TWIN_WEIGHTS:
            delta_w[n], new_m[n], new_v[n] = _adamw(weights[n], grad_w[n], given["m_" + n], given["v_" + n])
    return (loss, grad_x, *[grad_w[n] for n in TWIN_WEIGHTS], *[delta_w[n] for n in TWIN_WEIGHTS],
            *[new_m[n] for n in TWIN_WEIGHTS], *[new_v[n] for n in TWIN_WEIGHTS])
```

```python
import jax
import jax.numpy as jnp
from jax import lax
from jax.experimental import pallas as pl
from jax.experimental.pallas import tpu as pltpu

f32, bf16 = jnp.float32, jnp.bfloat16

N_DEV = 8
HEAD_DIM = 128
POOL_WINDOWS = (2, 4, 8, 16)
POOL_HALO = 16
CONV_HALO = 8
LN_EPS = 1e-5
ADAM_LR, ADAM_B1, ADAM_B2, ADAM_EPS, ADAM_WD, ADAM_STEP = 0.001, 0.9, 0.999, 1e-08, 0.01, 10
V7X_VMEM_BYTES = 64 * 1024 * 1024
VMEM_LIMIT_BYTES = V7X_VMEM_BYTES - 8 * 1024 * 1024
EXP_UNDERFLOW = -104.0
PACK_ALIGN = 8 * 128

NT_DIMS = (((1,), (1,)), ((), ()))
TN_DIMS = (((0,), (0,)), ((), ()))


def _pcall(body, **kw):
    return pl.pallas_call(body, **kw)


def _cp(*sem):
    return pltpu.CompilerParams(dimension_semantics=sem, vmem_limit_bytes=VMEM_LIMIT_BYTES)


def _row_tile(rows, cap):
    if rows <= cap:
        return rows
    best = None
    for t in range(8, cap + 1, 8):
        if rows % t == 0:
            best = t
    assert best is not None, rows
    return best


def _lane_tile(cols, cap):
    best = cols
    for t in range(128, min(cap, cols) + 1, 128):
        if cols % t == 0:
            best = t
    return best if cols > cap else cols


def mm_nn(a, b, out_dtype, name, out_cols=False, tm=512):
    M, K = a.shape
    J, _, N = b.shape
    tm = _row_tile(M, tm)

    def body(a_ref, b_ref, o_ref):
        o_ref[...] = jnp.dot(a_ref[...], b_ref[...], preferred_element_type=f32).astype(o_ref.dtype)

    if out_cols:
        out_spec, out_shape = pl.BlockSpec((tm, N), lambda j, i: (i, j)), (M, J * N)
    else:
        out_spec, out_shape = pl.BlockSpec((None, tm, N), lambda j, i: (j, i, 0)), (J, M, N)
    return _pcall(
        body, name=name, grid=(J, M // tm),
        in_specs=[pl.BlockSpec((tm, K), lambda j, i: (i, 0)), pl.BlockSpec((None, K, N), lambda j, i: (j, 0, 0))],
        out_specs=out_spec, out_shape=jax.ShapeDtypeStruct(out_shape, out_dtype),
        compiler_params=_cp("parallel", "parallel"))(a, b)


def mm_nn_sum(a, b, name, tm=1024):
    J, M, K = a.shape
    N = b.shape[2]
    tm = _row_tile(M, tm)

    def body(a_ref, b_ref, o_ref):
        j = pl.program_id(1)
        p = jnp.dot(a_ref[...], b_ref[...], preferred_element_type=f32)

        @pl.when(j == 0)
        def _():
            o_ref[...] = p

        @pl.when(j > 0)
        def _():
            o_ref[...] += p

    return _pcall(
        body, name=name, grid=(M // tm, J),
        in_specs=[pl.BlockSpec((None, tm, K), lambda i, j: (j, i, 0)), pl.BlockSpec((None, K, N), lambda i, j: (j, 0, 0))],
        out_specs=pl.BlockSpec((tm, N), lambda i, j: (i, 0)), out_shape=jax.ShapeDtypeStruct((M, N), f32),
        compiler_params=_cp("parallel", "arbitrary"))(a, b)


def mm_nt(a, b, out_dtype, name, tm=512):
    M, K = a.shape
    J, N, _ = b.shape
    tm = _row_tile(M, tm)

    def body(a_ref, b_ref, o_ref):
        o_ref[...] = lax.dot_general(a_ref[...], b_ref[...], NT_DIMS, preferred_element_type=f32).astype(o_ref.dtype)

    return _pcall(
        body, name=name, grid=(J, M // tm),
        in_specs=[pl.BlockSpec((tm, K), lambda j, i: (i, 0)), pl.BlockSpec((None, N, K), lambda j, i: (j, 0, 0))],
        out_specs=pl.BlockSpec((None, tm, N), lambda j, i: (j, i, 0)), out_shape=jax.ShapeDtypeStruct((J, M, N), out_dtype),
        compiler_params=_cp("parallel", "parallel"))(a, b)


def mm_nt_sum(a, b, add, alpha, name, a_cols=False, tm=512):
    J, N, K = b.shape
    M = a.shape[0] if a_cols else a.shape[1]
    tm = _row_tile(M, tm)
    nj = J

    def body(a_ref, b_ref, add_ref, o_ref):
        j = pl.program_id(1)
        p = lax.dot_general(a_ref[...], b_ref[...], NT_DIMS, preferred_element_type=f32)

        @pl.when(j == 0)
        def _():
            o_ref[...] = p

        @pl.when(j > 0)
        def _():
            o_ref[...] += p

        @pl.when(j == nj - 1)
        def _():
            o_ref[...] += alpha * add_ref[...]

    a_spec = pl.BlockSpec((tm, K), lambda i, j: (i, j)) if a_cols else pl.BlockSpec((None, tm, K), lambda i, j: (j, i, 0))
    return _pcall(
        body, name=name, grid=(M // tm, J),
        in_specs=[a_spec, pl.BlockSpec((None, N, K), lambda i, j: (j, 0, 0)), pl.BlockSpec((tm, N), lambda i, j: (i, 0))],
        out_specs=pl.BlockSpec((tm, N), lambda i, j: (i, 0)), out_shape=jax.ShapeDtypeStruct((M, N), f32),
        compiler_params=_cp("parallel", "arbitrary"))(a, b, add)


def mm_tn(a, b, out_dtype, name, b_cols=0, tm=1024, tk=512):
    JA, T, M = a.shape
    if b_cols:
        JB, N = b_cols, b.shape[1] // b_cols
    else:
        JB, _, N = b.shape
    J = max(JA, JB)
    tm = _lane_tile(M, tm)
    tk = _row_tile(T, tk)
    nk = T // tk

    def body(a_ref, b_ref, o_ref, acc):
        k = pl.program_id(2)
        p = lax.dot_general(a_ref[...], b_ref[...], TN_DIMS, preferred_element_type=f32)

        @pl.when(k == 0)
        def _():
            acc[...] = p

        @pl.when(k > 0)
        def _():
            acc[...] += p

        @pl.when(k == nk - 1)
        def _():
            o_ref[...] = acc[...].astype(o_ref.dtype)

    a_spec = pl.BlockSpec((None, tk, tm), (lambda j, i, k: (j, k, i)) if JA > 1 else (lambda j, i, k: (0, k, i)))
    if b_cols:
        b_spec = pl.BlockSpec((tk, N), lambda j, i, k: (k, j))
    else:
        b_spec = pl.BlockSpec((None, tk, N), (lambda j, i, k: (j, k, 0)) if JB > 1 else (lambda j, i, k: (0, k, 0)))
    return _pcall(
        body, name=name, grid=(J, M // tm, nk), in_specs=[a_spec, b_spec],
        out_specs=pl.BlockSpec((None, tm, N), lambda j, i, k: (j, i, 0)), out_shape=jax.ShapeDtypeStruct((J, M, N), out_dtype),
        scratch_shapes=[pltpu.VMEM((tm, N), f32)],
        compiler_params=_cp("parallel", "parallel", "arbitrary"))(a, b)


def ln_fwd(x, u, g, b, alpha, name, tm=256):
    T, D = x.shape
    tm = _row_tile(T, tm)

    def body(x_ref, u_ref, g_ref, b_ref, y_ref, yb_ref, xh_ref, rs_ref):
        s = alpha * x_ref[...] + u_ref[...]
        mu = jnp.mean(s, axis=-1, keepdims=True)
        c = s - mu
        var = jnp.mean(c * c, axis=-1, keepdims=True)
        r = lax.rsqrt(var + LN_EPS)
        xh = c * r
        y = xh * g_ref[...] + b_ref[...]
        y_ref[...] = y
        yb_ref[...] = y.astype(bf16)
        xh_ref[...] = xh
        rs_ref[...] = r

    row = pl.BlockSpec((tm, D), lambda i: (i, 0))
    vec = pl.BlockSpec((1, D), lambda i: (0, 0))
    return _pcall(
        body, name=name, grid=(T // tm,), in_specs=[row, row, vec, vec],
        out_specs=[row, row, row, pl.BlockSpec((tm, 1), lambda i: (i, 0))],
        out_shape=[jax.ShapeDtypeStruct((T, D), f32), jax.ShapeDtypeStruct((T, D), bf16),
                   jax.ShapeDtypeStruct((T, D), f32), jax.ShapeDtypeStruct((T, 1), f32)],
        compiler_params=_cp("parallel"))(x, u, g.reshape(1, D), b.reshape(1, D))


def ln_bwd(dy, xhat, rstd, g, name, tm=256):
    T, D = dy.shape
    tm = _row_tile(T, tm)

    def body(dy_ref, xh_ref, rs_ref, g_ref, ds_ref, dsb_ref, dg_ref, db_ref):
        i = pl.program_id(0)
        dy_t, xh = dy_ref[...], xh_ref[...]
        dxh = dy_t * g_ref[...]
        m1 = jnp.mean(dxh, axis=-1, keepdims=True)
        m2 = jnp.mean(dxh * xh, axis=-1, keepdims=True)
        ds = rs_ref[...] * (dxh - m1 - xh * m2)
        ds_ref[...] = ds
        dsb_ref[...] = ds.astype(bf16)
        pg = jnp.sum(dy_t * xh, axis=0, keepdims=True)
        pb = jnp.sum(dy_t, axis=0, keepdims=True)

        @pl.when(i == 0)
        def _():
            dg_ref[...] = pg
            db_ref[...] = pb

        @pl.when(i > 0)
        def _():
            dg_ref[...] += pg
            db_ref[...] += pb

    row = pl.BlockSpec((tm, D), lambda i: (i, 0))
    vec = pl.BlockSpec((1, D), lambda i: (0, 0))
    return _pcall(
        body, name=name, grid=(T // tm,), in_specs=[row, row, pl.BlockSpec((tm, 1), lambda i: (i, 0)), vec],
        out_specs=[row, row, vec, vec],
        out_shape=[jax.ShapeDtypeStruct((T, D), f32), jax.ShapeDtypeStruct((T, D), bf16),
                   jax.ShapeDtypeStruct((1, D), f32), jax.ShapeDtypeStruct((1, D), f32)],
        compiler_params=_cp("arbitrary"))(dy, xhat, rstd, g.reshape(1, D))


def loss_head(y, target, name, tm=256):
    T, D = y.shape
    tm = _row_tile(T, tm)
    nt = T // tm

    def body(y_ref, t_ref, dy_ref, l_ref, acc):
        i = pl.program_id(0)
        e = y_ref[...] - t_ref[...]
        dy_ref[...] = e / D
        p = jnp.sum(e * e, axis=0, keepdims=True)

        @pl.when(i == 0)
        def _():
            acc[...] = p

        @pl.when(i > 0)
        def _():
            acc[...] += p

        @pl.when(i == nt - 1)
        def _():
            l_ref[...] = jnp.full(l_ref.shape, 0.5 * jnp.sum(acc[...]) / D, f32)

    row = pl.BlockSpec((tm, D), lambda i: (i, 0))
    return _pcall(
        body, name=name, grid=(nt,), in_specs=[row, row],
        out_specs=[row, pl.BlockSpec((1, 128), lambda i: (0, 0))],
        out_shape=[jax.ShapeDtypeStruct((T, D), f32), jax.ShapeDtypeStruct((1, 128), f32)],
        scratch_shapes=[pltpu.VMEM((1, D), f32)], compiler_params=_cp("arbitrary"))(y, target)


def pool_fwd(x, w, scale, name, tm=256):
    T, D = x.shape
    G, C, _ = w.shape
    tm = _row_tile(T, tm)
    assert all(wd & (wd - 1) == 0 and wd - 1 <= POOL_HALO for wd in POOL_WINDOWS) and tm >= POOL_HALO

    def body(x_ref, w_ref, s_ref, u_ref, p_ref, halo):
        i = pl.program_id(0)

        @pl.when(i == 0)
        def _():
            halo[...] = jnp.zeros_like(halo)

        cur = x_ref[...]
        cat = jnp.concatenate([halo[...], cur], axis=0)
        halo[...] = cur[tm - POOL_HALO:, :]
        t1 = i * tm + lax.broadcasted_iota(jnp.int32, (tm, 1), 0) + 1
        for gi, wd in enumerate(POOL_WINDOWS):
            lo, hi = gi * C, (gi + 1) * C
            win = cat[:, lo:hi]
            sh = 1
            while sh < wd:
                win = win + pltpu.roll(win, sh, 0)
                sh *= 2
            cnt = jnp.minimum(t1, wd).astype(f32)
            pooled = (win[POOL_HALO:, :] / cnt - cur[:, lo:hi]).astype(bf16)
            p_ref[:, lo:hi] = pooled
            u_ref[:, lo:hi] = jnp.dot(pooled, w_ref[gi], preferred_element_type=f32) * s_ref[:, lo:hi]

    row = pl.BlockSpec((tm, D), lambda i: (i, 0))
    return _pcall(
        body, name=name, grid=(T // tm,),
        in_specs=[row, pl.BlockSpec((G, C, C), lambda i: (0, 0, 0)), pl.BlockSpec((1, D), lambda i: (0, 0))],
        out_specs=[row, row], out_shape=[jax.ShapeDtypeStruct((T, D), f32), jax.ShapeDtypeStruct((T, D), bf16)],
        scratch_shapes=[pltpu.VMEM((POOL_HALO, D), f32)], compiler_params=_cp("arbitrary"))(x, w, scale.reshape(1, D))


def pool_bwd(du, pooled, w, scale, alpha, name, tm=256):
    T, D = du.shape
    G, C, _ = w.shape
    tm = _row_tile(T, tm)
    nt = T // tm
    n = tm + POOL_HALO

    def body(du_ref, p_ref, w_ref, s_ref, dx_ref, dw_ref, dsc_ref, halo):
        i = pl.program_id(0)

        @pl.when(i == 0)
        def _():
            halo[...] = jnp.zeros_like(halo)
            dw_ref[...] = jnp.zeros_like(dw_ref)
            dsc_ref[...] = jnp.zeros_like(dsc_ref)

        t1 = (nt - 1 - i) * tm + lax.broadcasted_iota(jnp.int32, (tm, 1), 0) + 1
        for gi, wd in enumerate(POOL_WINDOWS):
            lo, hi = gi * C, (gi + 1) * C
            du_g, pb = du_ref[:, lo:hi], p_ref[:, lo:hi]
            yg = jnp.dot(pb, w_ref[gi], preferred_element_type=f32)
            dsc_ref[:, lo:hi] += jnp.sum(du_g * yg, axis=0, keepdims=True)
            dyg = (du_g * s_ref[:, lo:hi]).astype(bf16)
            dw_ref[gi] += lax.dot_general(pb, dyg, TN_DIMS, preferred_element_type=f32)
            dp = lax.dot_general(dyg, w_ref[gi], NT_DIMS, preferred_element_type=f32)
            e = dp / jnp.minimum(t1, wd).astype(f32)
            win = jnp.concatenate([e, halo[:, lo:hi]], axis=0)
            halo[:, lo:hi] = e[:POOL_HALO, :]
            sh = 1
            while sh < wd:
                win = win + pltpu.roll(win, n - sh, 0)
                sh *= 2
            dx_ref[:, lo:hi] = alpha * du_g + win[:tm, :] - dp

    row = pl.BlockSpec((tm, D), lambda i: (nt - 1 - i, 0))
    return _pcall(
        body, name=name, grid=(nt,),
        in_specs=[row, row, pl.BlockSpec((G, C, C), lambda i: (0, 0, 0)), pl.BlockSpec((1, D), lambda i: (0, 0))],
        out_specs=[row, pl.BlockSpec((G, C, C), lambda i: (0, 0, 0)), pl.BlockSpec((1, D), lambda i: (0, 0))],
        out_shape=[jax.ShapeDtypeStruct((T, D), f32), jax.ShapeDtypeStruct((G, C, C), f32), jax.ShapeDtypeStruct((1, D), f32)],
        scratch_shapes=[pltpu.VMEM((POOL_HALO, D), f32)], compiler_params=_cp("arbitrary"))(du, pooled, w, scale.reshape(1, D))


def _sigmoid(x):
    return 1.0 / (1.0 + jnp.exp(-x))


def _causal_conv(cur, prev, w_ref, b_ref):
    cat = jnp.concatenate([prev, cur], axis=0)
    h1 = pltpu.roll(cat, 1, 0)[CONV_HALO:, :]
    h2 = pltpu.roll(cat, 2, 0)[CONV_HALO:, :]
    out = b_ref[...] + w_ref[0:1, :] * h2
    out = out + w_ref[1:2, :] * h1
    out = out + w_ref[2:3, :] * cur
    return out, h1, h2


def gate_fwd(h, cw, cb, name, tm=256):
    J2, T, FB = h.shape
    J = J2 // 2
    tm = _row_tile(T, tm)

    def body(hg_ref, hv_ref, wg_ref, wv_ref, bg_ref, bv_ref, a_ref, halo_g, halo_v):
        i = pl.program_id(1)

        @pl.when(i == 0)
        def _():
            halo_g[...] = jnp.zeros_like(halo_g)
            halo_v[...] = jnp.zeros_like(halo_v)

        hg, hv = hg_ref[...], hv_ref[...]
        gate, _, _ = _causal_conv(hg, halo_g[...], wg_ref, bg_ref)
        val, _, _ = _causal_conv(hv, halo_v[...], wv_ref, bv_ref)
        halo_g[...] = hg[tm - CONV_HALO:, :]
        halo_v[...] = hv[tm - CONV_HALO:, :]
        a_ref[...] = (gate * _sigmoid(gate) * val).astype(bf16)

    def blk(rows, off):
        return pl.BlockSpec((None, rows, FB), lambda j, i: (j + off, i if rows == tm else 0, 0))

    return _pcall(
        body, name=name, grid=(J, T // tm),
        in_specs=[blk(tm, 0), blk(tm, J), blk(3, 0), blk(3, J), blk(1, 0), blk(1, J)],
        out_specs=pl.BlockSpec((None, tm, FB), lambda j, i: (j, i, 0)), out_shape=jax.ShapeDtypeStruct((J, T, FB), bf16),
        scratch_shapes=[pltpu.VMEM((CONV_HALO, FB), f32), pltpu.VMEM((CONV_HALO, FB), f32)],
        compiler_params=_cp("parallel", "arbitrary"))(h, h, cw, cw, cb, cb)


def gate_bwd(h, da, cw, cb, name, tm=256):
    J2, T, FB = h.shape
    J = J2 // 2
    tm = _row_tile(T, tm)
    nt = T // tm
    n = tm + CONV_HALO

    def body(hg_ref, hv_ref, pg_ref, pv_ref, da_ref, wg_ref, wv_ref, bg_ref, bv_ref,
             dhg_ref, dhv_ref, dwg_ref, dwv_ref, dbg_ref, dbv_ref, halo_g, halo_v):
        i = pl.program_id(1)

        @pl.when(i == 0)
        def _():
            for r in (halo_g, halo_v, dwg_ref, dwv_ref, dbg_ref, dbv_ref):
                r[...] = jnp.zeros_like(r)

        has_prev = (i < nt - 1).astype(f32)
        hg, hv = hg_ref[...], hv_ref[...]
        gate, g1, g2 = _causal_conv(hg, pg_ref[...] * has_prev, wg_ref, bg_ref)
        val, v1, v2 = _causal_conv(hv, pv_ref[...] * has_prev, wv_ref, bv_ref)
        sg = _sigmoid(gate)
        da_t = da_ref[...]
        dval = da_t * (gate * sg)
        dgate = da_t * val * (sg * (1.0 + gate * (1.0 - sg)))

        def back(d, cur, s1, s2, w_ref, dw_ref, db_ref, halo, dh_ref):
            db_ref[...] += jnp.sum(d, axis=0, keepdims=True)
            dw_ref[0:1, :] += jnp.sum(d * s2, axis=0, keepdims=True)
            dw_ref[1:2, :] += jnp.sum(d * s1, axis=0, keepdims=True)
            dw_ref[2:3, :] += jnp.sum(d * cur, axis=0, keepdims=True)
            cat = jnp.concatenate([d, halo[...]], axis=0)
            halo[...] = d[:CONV_HALO, :]
            d1 = pltpu.roll(cat, n - 1, 0)[:tm, :]
            d2 = pltpu.roll(cat, n - 2, 0)[:tm, :]
            dh = w_ref[2:3, :] * d + w_ref[1:2, :] * d1 + w_ref[0:1, :] * d2
            dh_ref[...] = dh.astype(bf16)

        back(dgate, hg, g1, g2, wg_ref, dwg_ref, dbg_ref, halo_g, dhg_ref)
        back(dval, hv, v1, v2, wv_ref, dwv_ref, dbv_ref, halo_v, dhv_ref)

    def rows(off):
        return pl.BlockSpec((None, tm, FB), lambda j, i: (j + off, nt - 1 - i, 0))

    def prev(off):
        return pl.BlockSpec((None, CONV_HALO, FB), lambda j, i: (j + off, jnp.maximum((nt - 1 - i) * (tm // CONV_HALO) - 1, 0), 0))

    def small(r, off):
        return pl.BlockSpec((None, r, FB), lambda j, i: (j + off, 0, 0))

    dhg, dhv, dwg, dwv, dbg, dbv = _pcall(
        body, name=name, grid=(J, nt),
        in_specs=[rows(0), rows(J), prev(0), prev(J), rows(0), small(3, 0), small(3, J), small(1, 0), small(1, J)],
        out_specs=[rows(0), rows(0), small(3, 0), small(3, 0), small(1, 0), small(1, 0)],
        out_shape=[jax.ShapeDtypeStruct((J, T, FB), bf16), jax.ShapeDtypeStruct((J, T, FB), bf16),
                   jax.ShapeDtypeStruct((J, 3, FB), f32), jax.ShapeDtypeStruct((J, 3, FB), f32),
                   jax.ShapeDtypeStruct((J, 1, FB), f32), jax.ShapeDtypeStruct((J, 1, FB), f32)],
        scratch_shapes=[pltpu.VMEM((CONV_HALO, FB), f32), pltpu.VMEM((CONV_HALO, FB), f32)],
        compiler_params=_cp("parallel", "arbitrary"))(h, h, h, h, da, cw, cw, cb, cb)
    return (jnp.concatenate([dhg, dhv], axis=0), jnp.concatenate([dwg, dwv], axis=0), jnp.concatenate([dbg, dbv], axis=0))


def _split_dot(x, tri):
    hi = x.astype(bf16)
    lo = (x - hi.astype(f32)).astype(bf16)
    return jnp.dot(hi, tri, preferred_element_type=f32) + jnp.dot(lo, tri, preferred_element_type=f32)


def _attn_block(q, kblk, k0, q_pos, carry_r, tri_incl, scale):
    tq, tk = q.shape[0], kblk.shape[0]
    z = lax.dot_general(q, kblk, NT_DIMS, preferred_element_type=f32) * scale
    mask = (k0 + lax.broadcasted_iota(jnp.int32, (tq, tk), 1)) < q_pos
    e = jnp.exp(-jnp.abs(z))
    lsm = jnp.where(mask, -(jnp.maximum(z, 0.0) + jnp.log(1.0 + e)), 0.0)
    rest = _split_dot(lsm, tri_incl) + carry_r
    a = jnp.where(mask, jnp.exp(z + rest), 0.0)
    return mask, z, e, lsm, a


def _tri(tk, strict):
    r = lax.broadcasted_iota(jnp.int32, (tk, tk), 0)
    c = lax.broadcasted_iota(jnp.int32, (tk, tk), 1)
    return ((r > c) if strict else (r >= c)).astype(bf16)


def attn_fwd(qkv, name, tq=256, tk=128):
    T, D3 = qkv.shape
    D = D3 // 3
    H = D // HEAD_DIM
    tq = _row_tile(T, tq)
    tk = min(tk, tq)
    scale = HEAD_DIM ** -0.5

    def body(q_ref, k_ref, v_ref, o_ref):
        i = pl.program_id(1)
        q = q_ref[...]
        q_pos = i * tq + lax.broadcasted_iota(jnp.int32, (tq, tk), 0)
        tri = _tri(tk, False)
        nkb = (i + 1) * (tq // tk)

        def cond(c):
            return jnp.logical_and(c[0] < nkb, jnp.max(c[1]) > EXP_UNDERFLOW)

        def step(c):
            it, r, acc = c
            k0 = pl.multiple_of((nkb - 1 - it) * tk, tk)
            _, _, _, lsm, a = _attn_block(q, k_ref[pl.ds(k0, tk), :], k0, q_pos, r, tri, scale)
            acc = acc + jnp.dot(a.astype(bf16), v_ref[pl.ds(k0, tk), :], preferred_element_type=f32)
            return it + 1, r + jnp.sum(lsm, axis=-1, keepdims=True), acc

        _, _, acc = lax.while_loop(cond, step, (jnp.int32(0), jnp.zeros((tq, 1), f32), jnp.zeros((tq, HEAD_DIM), f32)))
        o_ref[...] = acc.astype(bf16)

    return _pcall(
        body, name=name, grid=(H, T // tq),
        in_specs=[pl.BlockSpec((tq, HEAD_DIM), lambda h, i: (i, h)),
                  pl.BlockSpec((T, HEAD_DIM), lambda h, i: (0, H + h)),
                  pl.BlockSpec((T, HEAD_DIM), lambda h, i: (0, 2 * H + h))],
        out_specs=pl.BlockSpec((tq, HEAD_DIM), lambda h, i: (i, h)), out_shape=jax.ShapeDtypeStruct((T, D), bf16),
        compiler_params=_cp("parallel", "arbitrary"))(qkv, qkv, qkv)


def attn_bwd(qkv, do, name, tq=256, tk=128):
    T, D3 = qkv.shape
    D = D3 // 3
    H = D // HEAD_DIM
    tq = _row_tile(T, tq)
    tk = min(tk, tq)
    nq = T // tq
    scale = HEAD_DIM ** -0.5

    def body(q_ref, k_ref, v_ref, do_ref, dq_ref, dk_ref, dv_ref, dk_acc, dv_acc):
        i = pl.program_id(1)

        @pl.when(i == 0)
        def _():
            dk_acc[...] = jnp.zeros_like(dk_acc)
            dv_acc[...] = jnp.zeros_like(dv_acc)

        q, do_t = q_ref[...], do_ref[...]
        q_pos = i * tq + lax.broadcasted_iota(jnp.int32, (tq, tk), 0)
        tri_incl, tri_excl = _tri(tk, False), _tri(tk, True)
        nkb = (i + 1) * (tq // tk)
        zero = jnp.zeros((tq, 1), f32)

        def cond(c):
            return jnp.logical_and(c[0] < nkb, jnp.max(c[1]) > EXP_UNDERFLOW)

        def weights(it, r):
            k0 = pl.multiple_of((nkb - 1 - it) * tk, tk)
            mask, z, e, lsm, a = _attn_block(q, k_ref[pl.ds(k0, tk), :], k0, q_pos, r, tri_incl, scale)
            g = a * lax.dot_general(do_t, v_ref[pl.ds(k0, tk), :], NT_DIMS, preferred_element_type=f32)
            return k0, mask, z, e, lsm, a, g

        def total_step(c):
            it, r, tot = c
            _, _, _, _, lsm, _, g = weights(it, r)
            return it + 1, r + jnp.sum(lsm, axis=-1, keepdims=True), tot + jnp.sum(g, axis=-1, keepdims=True)

        _, _, g_total = lax.while_loop(cond, total_step, (jnp.int32(0), zero, zero))

        def grad_step(c):
            it, r, run, dq = c
            k0, mask, z, e, lsm, a, g = weights(it, r)
            prefix = g_total - (run + _split_dot(g, tri_excl))
            sig = jnp.where(z >= 0.0, 1.0, e) / (1.0 + e)
            dz = (jnp.where(mask, g - sig * prefix, 0.0) * scale).astype(bf16)
            dq = dq + jnp.dot(dz, k_ref[pl.ds(k0, tk), :], preferred_element_type=f32)
            dk_acc[pl.ds(k0, tk), :] += lax.dot_general(dz, q, TN_DIMS, preferred_element_type=f32)
            dv_acc[pl.ds(k0, tk), :] += lax.dot_general(a.astype(bf16), do_t, TN_DIMS, preferred_element_type=f32)
            return it + 1, r + jnp.sum(lsm, axis=-1, keepdims=True), run + jnp.sum(g, axis=-1, keepdims=True), dq

        _, _, _, dq = lax.while_loop(cond, grad_step, (jnp.int32(0), zero, zero, jnp.zeros((tq, HEAD_DIM), f32)))
        dq_ref[...] = dq.astype(bf16)

        @pl.when(i == nq - 1)
        def _():
            dk_ref[...] = dk_acc[...].astype(bf16)
            dv_ref[...] = dv_acc[...].astype(bf16)

    qblk = pl.BlockSpec((tq, HEAD_DIM), lambda h, i: (i, h))
    head = pl.BlockSpec((T, HEAD_DIM), lambda h, i: (0, h))
    return _pcall(
        body, name=name, grid=(H, nq),
        in_specs=[qblk, pl.BlockSpec((T, HEAD_DIM), lambda h, i: (0, H + h)),
                  pl.BlockSpec((T, HEAD_DIM), lambda h, i: (0, 2 * H + h)), qblk],
        out_specs=[qblk, head, head], out_shape=[jax.ShapeDtypeStruct((T, D), bf16)] * 3,
        scratch_shapes=[pltpu.VMEM((T, HEAD_DIM), f32), pltpu.VMEM((T, HEAD_DIM), f32)],
        compiler_params=_cp("parallel", "arbitrary"))(qkv, qkv, qkv, do)


def _mesh_place():
    x, y, c = lax.axis_index("x"), lax.axis_index("y"), lax.axis_index("c")
    peers = []
    for k in range(1, N_DEV):
        px = 1 - x if k & 4 else x
        py = 1 - y if k & 2 else y
        pc = 1 - c if k & 1 else c
        peers.append(((px, py, pc), 4 * px + 2 * py + pc))
    return 4 * x + 2 * y + c, peers


def _exchange(arrs, scatter, name):
    n = len(arrs)
    n_peer = N_DEV - 1

    def body(*refs):
        ins, outs = refs[:n], refs[n:2 * n]
        send, recv, loc = refs[2 * n:]
        me, peers = _mesh_place()

        def src(a, idx):
            return ins[a].at[idx] if scatter else ins[a]

        local = [pltpu.make_async_copy(src(a, me), outs[a].at[me], loc.at[a]) for a in range(n)]
        for cp in local:
            cp.start()
        for a in range(n):
            for k, (dev, idx) in enumerate(peers):
                pltpu.make_async_remote_copy(
                    src_ref=src(a, idx), dst_ref=outs[a].at[me], send_sem=send.at[a * n_peer + k],
                    recv_sem=recv.at[a * n_peer + k], device_id=dev, device_id_type=pl.DeviceIdType.MESH).start()
        for a in range(n):
            for k, (dev, idx) in enumerate(peers):
                pltpu.make_async_remote_copy(
                    src_ref=src(a, idx), dst_ref=outs[a].at[idx], send_sem=send.at[a * n_peer + k],
                    recv_sem=recv.at[a * n_peer + k], device_id=dev, device_id_type=pl.DeviceIdType.MESH).wait()
        for cp in local:
            cp.wait()

    any_spec = pl.BlockSpec(memory_space=pl.ANY)
    out_shape = [jax.ShapeDtypeStruct(a.shape if scatter else (N_DEV, *a.shape), a.dtype) for a in arrs]
    return _pcall(
        body, name=name, in_specs=[any_spec] * n, out_specs=[any_spec] * n, out_shape=out_shape,
        scratch_shapes=[pltpu.SemaphoreType.DMA((n * n_peer,)), pltpu.SemaphoreType.DMA((n * n_peer,)),
                        pltpu.SemaphoreType.DMA((n,))])(*arrs)


def all_gather(shards, name):
    return _exchange(shards, False, name)


def scatter_partials(partials, name):
    return _exchange(partials, True, name)


def sum_partials(p, name, tr=256):
    K, R, C = p.shape
    tr = _row_tile(R, tr)

    def body(p_ref, o_ref):
        g = p_ref[0].astype(f32)
        for k in range(1, K):
            g = g + p_ref[k].astype(f32)
        o_ref[...] = g

    return _pcall(
        body, name=name, grid=(R // tr,), in_specs=[pl.BlockSpec((K, tr, C), lambda i: (0, i, 0))],
        out_specs=pl.BlockSpec((tr, C), lambda i: (i, 0)), out_shape=jax.ShapeDtypeStruct((R, C), f32),
        compiler_params=_cp("parallel"))(p)


def adamw(p, w, m, v, name, tr=256):
    K, R, C = p.shape
    tr = _row_tile(R, tr)
    c1 = 1.0 - ADAM_B1 ** ADAM_STEP
    c2 = 1.0 - ADAM_B2 ** ADAM_STEP

    def body(p_ref, w_ref, m_ref, v_ref, g_ref, d_ref, nm_ref, nv_ref):
        g = p_ref[0].astype(f32)
        for k in range(1, K):
            g = g + p_ref[k].astype(f32)
        nm = ADAM_B1 * m_ref[...] + (1.0 - ADAM_B1) * g
        nv = ADAM_B2 * v_ref[...] + (1.0 - ADAM_B2) * (g * g)
        g_ref[...] = g
        nm_ref[...] = nm
        nv_ref[...] = nv
        d_ref[...] = -ADAM_LR * ((nm / c1) / (jnp.sqrt(nv / c2) + ADAM_EPS) + ADAM_WD * w_ref[...])

    row = pl.BlockSpec((tr, C), lambda i: (i, 0))
    return _pcall(
        body, name=name, grid=(R // tr,), in_specs=[pl.BlockSpec((K, tr, C), lambda i: (0, i, 0)), row, row, row],
        out_specs=[row] * 4, out_shape=[jax.ShapeDtypeStruct((R, C), f32)] * 4, compiler_params=_cp("parallel"))(p, w, m, v)


def _pack(parts):
    flat, slices, off = [], [], 0
    for a in parts:
        nel = a.size
        pad = -nel % PACK_ALIGN
        flat.append(jnp.pad(a.reshape(-1).astype(f32), (0, pad)))
        slices.append((off, nel, a.shape))
        off += nel + pad
    return jnp.concatenate(flat).reshape(-1, 128), slices


def _unpack(packed, slices):
    flat = packed.reshape(-1)
    return [flat[off:off + nel].reshape(shape) for off, nel, shape in slices]


def kernel(x, pool_w, pool_scale, attn_w_qkv, attn_w_o, ffn_w_up, ffn_conv_w, ffn_conv_b, ffn_w_down, ln_mix_g, ln_mix_b, ln_ffn_g, ln_ffn_b, loss_target, m_pool_w, m_pool_scale, m_attn_w_qkv, m_attn_w_o, m_ffn_w_up, m_ffn_conv_w, m_ffn_conv_b, m_ffn_w_down, m_ln_mix_g, m_ln_mix_b, m_ln_ffn_g, m_ln_ffn_b, v_pool_w, v_pool_scale, v_attn_w_qkv, v_attn_w_o, v_ffn_w_up, v_ffn_conv_w, v_ffn_conv_b, v_ffn_w_down, v_ln_mix_g, v_ln_mix_b, v_ln_ffn_g, v_ln_ffn_b):
    _, T, D = x.shape
    depth = ln_mix_g.shape[0]
    alpha = (2.0 * depth) ** 0.25
    G, CS, C = pool_w.shape[1:]
    FB = ffn_w_up.shape[2]
    JH = N_DEV // 2
    me = 4 * lax.axis_index("x") + 2 * lax.axis_index("y") + lax.axis_index("c")

    pool_b, qkv_b, wo_b = pool_w.astype(bf16), attn_w_qkv.astype(bf16), attn_w_o.astype(bf16)
    up_b, down_b = ffn_w_up.astype(bf16), ffn_w_down.astype(bf16)
    weights = []
    for l in range(depth):
        j = l // 2
        mixer = [pool_b[j]] if l % 2 == 0 else [qkv_b[j], wo_b[j]]
        got = all_gather(mixer + [up_b[l], down_b[l], ffn_conv_w[l]], f"gather_l{l}")
        wl = {"up": got[-3], "down": got[-2].reshape(JH, FB, D), "cw": got[-1],
              "cb": ffn_conv_b[l].reshape(N_DEV, 1, FB)}
        if l % 2 == 0:
            wl["pool"] = got[0].transpose(1, 0, 2, 3).reshape(G, C, C)
        else:
            wl["qkv"], wl["wo"] = got[0], got[1].reshape(1, D, D)
        weights.append(wl)

    xf = x[0]
    xb = xf.astype(bf16)
    saved = []
    for l, wl in enumerate(weights):
        j = l // 2
        sv = {}
        if l % 2 == 0:
            u, sv["pooled"] = pool_fwd(xf, wl["pool"], pool_scale[j], f"pool_fwd_l{l}")
        else:
            sv["xin_b"] = xb
            sv["qkv"] = mm_nn(xb, wl["qkv"], bf16, f"qkv_l{l}", out_cols=True)
            sv["o"] = attn_fwd(sv["qkv"], f"attn_fwd_l{l}")
            u = mm_nn(sv["o"], wl["wo"], f32, f"attn_out_l{l}")[0]
        x1, sv["x1_b"], sv["xhat1"], sv["rstd1"] = ln_fwd(xf, u, ln_mix_g[l], ln_mix_b[l], alpha, f"ln_mix_l{l}")
        sv["h"] = mm_nn(sv["x1_b"], wl["up"], f32, f"ffn_up_l{l}")
        sv["a"] = gate_fwd(sv["h"], wl["cw"], wl["cb"], f"ffn_gate_l{l}")
        f = mm_nn_sum(sv["a"], wl["down"], f"ffn_down_l{l}")
        xf, xb, sv["xhat2"], sv["rstd2"] = ln_fwd(x1, f, ln_ffn_g[l], ln_ffn_b[l], alpha, f"ln_ffn_l{l}")
        saved.append(sv)

    dx, loss_row = loss_head(xf, loss_target[0], "loss_head")

    grads = [None] * depth
    for l in reversed(range(depth)):
        wl, sv, j, gl = weights[l], saved[l], l // 2, {}
        ds2, ds2_b, gl["ln_ffn_g"], gl["ln_ffn_b"] = ln_bwd(dx, sv["xhat2"], sv["rstd2"], ln_ffn_g[l], f"ln_ffn_bwd_l{l}")
        da = mm_nt(ds2_b, wl["down"], f32, f"ffn_down_dx_l{l}")
        gl["down"] = mm_tn(sv["a"], ds2_b[None], bf16, f"ffn_down_dw_l{l}")
        dh, gl["cw"], gl["cb"] = gate_bwd(sv["h"], da, wl["cw"], wl["cb"], f"ffn_gate_bwd_l{l}")
        gl["up"] = mm_tn(sv["x1_b"][None], dh, bf16, f"ffn_up_dw_l{l}")
        dx1 = mm_nt_sum(dh, wl["up"], ds2, alpha, f"ffn_up_dx_l{l}")
        ds1, ds1_b, gl["ln_mix_g"], gl["ln_mix_b"] = ln_bwd(dx1, sv["xhat1"], sv["rstd1"], ln_mix_g[l], f"ln_mix_bwd_l{l}")
        if l % 2 == 0:
            dx, gl["pool"], gl["pool_scale"] = pool_bwd(ds1, sv["pooled"], wl["pool"], pool_scale[j], alpha, f"pool_bwd_l{l}")
        else:
            do = mm_nt(ds1_b, wl["wo"], bf16, f"attn_out_dx_l{l}")[0]
            gl["wo"] = mm_tn(sv["o"][None], ds1_b[None], bf16, f"attn_out_dw_l{l}")
            dqkv = jnp.concatenate(attn_bwd(sv["qkv"], do, f"attn_bwd_l{l}"), axis=1)
            gl["qkv"] = mm_tn(sv["xin_b"][None], dqkv, bf16, f"qkv_dw_l{l}", b_cols=N_DEV)
            dx = mm_nt_sum(dqkv, wl["qkv"], ds1, alpha, f"qkv_dx_l{l}", a_cols=True)
        grads[l] = gl
    grad_x = dx[None]

    def stack_update(name, w, m, v, landed):
        shard = w.shape[1:]
        rows = w[0].size // shard[-1]
        outs = [adamw(landed[i].reshape(N_DEV, rows, shard[-1]), w[i].reshape(rows, -1), m[i].reshape(rows, -1),
                      v[i].reshape(rows, -1), f"adamw_{name}_{i}") for i in range(w.shape[0])]
        return [jnp.stack([o[t].reshape(shard) for o in outs]) for t in range(4)]

    landed = {"pool": [], "qkv": [], "wo": [], "up": [], "down": []}
    for l in range(depth):
        gl = grads[l]
        if l % 2 == 0:
            mixer = [gl["pool"].reshape(G, N_DEV, CS, C).transpose(1, 0, 2, 3).astype(bf16)]
        else:
            mixer = [gl["qkv"], gl["wo"].reshape(N_DEV, D // N_DEV, D)]
        got = scatter_partials(mixer + [gl["up"], gl["down"].reshape(N_DEV, FB // 2, D)], f"scatter_l{l}")
        if l % 2 == 0:
            landed["pool"].append(got[0])
        else:
            landed["qkv"].append(got[0])
            landed["wo"].append(got[1])
        landed["up"].append(got[-2])
        landed["down"].append(got[-1])

    big = {
        "pool_w": stack_update("pool_w", pool_w, m_pool_w, v_pool_w, landed["pool"]),
        "attn_w_qkv": stack_update("attn_w_qkv", attn_w_qkv, m_attn_w_qkv, v_attn_w_qkv, landed["qkv"]),
        "attn_w_o": stack_update("attn_w_o", attn_w_o, m_attn_w_o, v_attn_w_o, landed["wo"]),
        "ffn_w_up": stack_update("ffn_w_up", ffn_w_up, m_ffn_w_up, v_ffn_w_up, landed["up"]),
        "ffn_w_down": stack_update("ffn_w_down", ffn_w_down, m_ffn_w_down, v_ffn_w_down, landed["down"]),
    }

    def per_layer(key):
        return jnp.stack([grads[l][key].reshape(-1) for l in range(depth)])

    small_parts = [per_layer("ln_mix_g"), per_layer("ln_mix_b"), per_layer("ln_ffn_g"), per_layer("ln_ffn_b"),
                   jnp.stack([grads[l]["pool_scale"].reshape(-1) for l in range(0, depth, 2)]),
                   per_layer("cb"), jnp.stack([grads[l]["cw"] for l in range(depth)]), loss_row[0, :1]]
    packed, slices = _pack(small_parts)
    total = sum_partials(all_gather([packed], "gather_small")[0], "sum_small")
    g_mix_g, g_mix_b, g_ffn_g, g_ffn_b, g_scale, g_cb, g_cw_all, loss = _unpack(total, slices)
    g_cw = lax.dynamic_index_in_dim(g_cw_all, me, axis=1, keepdims=False)

    small_w = [ln_mix_g, ln_mix_b, ln_ffn_g, ln_ffn_b, pool_scale, ffn_conv_b, ffn_conv_w]
    small_m = [m_ln_mix_g, m_ln_mix_b, m_ln_ffn_g, m_ln_ffn_b, m_pool_scale, m_ffn_conv_b, m_ffn_conv_w]
    small_v = [v_ln_mix_g, v_ln_mix_b, v_ln_ffn_g, v_ln_ffn_b, v_pool_scale, v_ffn_conv_b, v_ffn_conv_w]
    small_g = [g_mix_g, g_mix_b, g_ffn_g, g_ffn_b, g_scale, g_cb.reshape(ffn_conv_b.shape), g_cw]
    pw, wslices = _pack(small_w)
    res = adamw(_pack(small_g)[0][None], pw, _pack(small_m)[0], _pack(small_v)[0], "adamw_small")
    small = dict(zip(["ln_mix_g", "ln_mix_b", "ln_ffn_g", "ln_ffn_b", "pool_scale", "ffn_conv_b", "ffn_conv_w"],
                     zip(*[_unpack(r, wslices) for r in res])))

    order = ["pool_w", "pool_scale", "attn_w_qkv", "attn_w_o", "ffn_w_up", "ffn_conv_w", "ffn_conv_b", "ffn_w_down",
             "ln_mix_g", "ln_mix_b", "ln_ffn_g", "ln_ffn_b"]
    table = {**big, **{k: list(val) for k, val in small.items()}}
    outs = [loss.reshape(()), grad_x]
    for t in range(4):
        outs += [table[name][t] for name in order]
    return tuple(outs)
```

```python
import jax
import jax.numpy as jnp
from jax import lax
from jax.experimental import pallas as pl
from jax.experimental.pallas import tpu as pltpu

f32, bf16 = jnp.float32, jnp.bfloat16

N_DEV = 8
HEAD_DIM = 128
POOL_WINDOWS = (2, 4, 8, 16)
POOL_HALO = 16
CONV_HALO = 8
LN_EPS = 1e-5
ADAM_LR, ADAM_B1, ADAM_B2, ADAM_EPS, ADAM_WD, ADAM_STEP = 0.001, 0.9, 0.999, 1e-08, 0.01, 10
V7X_VMEM_BYTES = 64 * 1024 * 1024
VMEM_LIMIT_BYTES = V7X_VMEM_BYTES - 8 * 1024 * 1024
MM_VMEM_BUDGET_BYTES = 40 * 1024 * 1024
EXP_UNDERFLOW = -104.0
PACK_ALIGN = 8 * 128

NT_DIMS = (((1,), (1,)), ((), ()))
TN_DIMS = (((0,), (0,)), ((), ()))


def _pcall(body, **kw):
    return pl.pallas_call(body, **kw)


def _cp(*sem):
    return pltpu.CompilerParams(dimension_semantics=sem, vmem_limit_bytes=VMEM_LIMIT_BYTES)


def _row_tile(rows, cap):
    if rows <= cap:
        return rows
    best = None
    for t in range(8, cap + 1, 8):
        if rows % t == 0:
            best = t
    assert best is not None, rows
    return best


def _lane_tile(cols, cap):
    best = cols
    for t in range(128, min(cap, cols) + 1, 128):
        if cols % t == 0:
            best = t
    return best if cols > cap else cols


N_PEER = N_DEV - 1


def _mesh_place():
    x, y, c = lax.axis_index("x"), lax.axis_index("y"), lax.axis_index("c")
    peers = []
    for k in range(1, N_DEV):
        px = 1 - x if k & 4 else x
        py = 1 - y if k & 2 else y
        pc = 1 - c if k & 1 else c
        peers.append(((px, py, pc), 4 * px + 2 * py + pc))
    return 4 * x + 2 * y + c, peers


def _exchange_copies(ins, outs, send, recv, loc, place, scatter):
    me, peers = place

    def src(a, idx):
        return ins[a].at[idx] if scatter else ins[a]

    local, starts, waits = [], [], []
    for a in range(len(ins)):
        local.append(pltpu.make_async_copy(src(a, me), outs[a].at[me], loc.at[a]))
        for k, (dev, idx) in enumerate(peers):
            sems = dict(send_sem=send.at[a * N_PEER + k], recv_sem=recv.at[a * N_PEER + k],
                        device_id=dev, device_id_type=pl.DeviceIdType.MESH)
            starts.append(pltpu.make_async_remote_copy(src_ref=src(a, idx), dst_ref=outs[a].at[me], **sems))
            waits.append(pltpu.make_async_remote_copy(src_ref=src(a, idx), dst_ref=outs[a].at[idx], **sems))
    return local, starts, waits


def _exchange_shapes(arrs, scatter):
    return [jax.ShapeDtypeStruct(a.shape if scatter else (N_DEV, *a.shape), a.dtype) for a in arrs]


def _exchange_sems(n):
    return [pltpu.SemaphoreType.DMA((n * N_PEER,)), pltpu.SemaphoreType.DMA((n * N_PEER,)), pltpu.SemaphoreType.DMA((n,))]


def _exchange(arrs, scatter, name):
    n = len(arrs)

    def body(*refs):
        local, starts, waits = _exchange_copies(refs[:n], refs[n:2 * n], *refs[2 * n:], _mesh_place(), scatter)
        for cp in local + starts:
            cp.start()
        for cp in waits + local:
            cp.wait()

    any_spec = pl.BlockSpec(memory_space=pl.ANY)
    return _pcall(body, name=name, in_specs=[any_spec] * n, out_specs=[any_spec] * n,
                  out_shape=_exchange_shapes(arrs, scatter), scratch_shapes=_exchange_sems(n))(*arrs)


def all_gather(shards, name):
    return _exchange(shards, False, name)


def scatter_partials(partials, name):
    return _exchange(partials, True, name)


def _run(body, args, exch, *, name, grid, in_specs, out_specs, out_shape, scratch_shapes=(), sem):
    scratch_shapes = list(scratch_shapes)
    if exch is None:
        return _pcall(body, name=name, grid=grid, in_specs=in_specs, out_specs=out_specs, out_shape=out_shape,
                      scratch_shapes=scratch_shapes, compiler_params=_cp(*sem))(*args)
    arrs, scatter = exch
    single = not isinstance(out_shape, (list, tuple))
    out_specs, out_shape = ([out_specs], [out_shape]) if single else (list(out_specs), list(out_shape))
    n, na, nb, ns = len(arrs), len(in_specs), len(out_specs), len(scratch_shapes)

    def hosted(*refs):
        ins, xin = refs[:na], refs[na:na + n]
        outs, xout = refs[na + n:na + n + nb], refs[na + n + nb:na + 2 * n + nb]
        scr, sems = refs[na + 2 * n + nb:na + 2 * n + nb + ns], refs[na + 2 * n + nb + ns:]
        first, last = None, None
        for d, size in enumerate(grid):
            pid = pl.program_id(d)
            first = (pid == 0) if first is None else jnp.logical_and(first, pid == 0)
            last = (pid == size - 1) if last is None else jnp.logical_and(last, pid == size - 1)
        local, starts, waits = _exchange_copies(xin, xout, *sems, _mesh_place(), scatter)

        @pl.when(first)
        def _():
            for cp in local + starts:
                cp.start()

        body(*ins, *outs, *scr)

        @pl.when(last)
        def _():
            for cp in waits + local:
                cp.wait()

    any_spec = pl.BlockSpec(memory_space=pl.ANY)
    res = _pcall(hosted, name=name, grid=grid, in_specs=list(in_specs) + [any_spec] * n,
                 out_specs=out_specs + [any_spec] * n, out_shape=out_shape + _exchange_shapes(arrs, scatter),
                 scratch_shapes=scratch_shapes + _exchange_sems(n),
                 compiler_params=_cp(*(["arbitrary"] * len(grid))))(*args, *arrs)
    return (res[0] if single else list(res[:nb])), list(res[nb:])


def mm_nn(a, b, out_dtype, name, out_cols=False, exch=None, tm=512):
    M, K = a.shape
    J, _, N = b.shape
    tm = _row_tile(M, tm)

    def body(a_ref, b_ref, o_ref):
        o_ref[...] = jnp.dot(a_ref[...], b_ref[...], preferred_element_type=f32).astype(o_ref.dtype)

    if out_cols:
        out_spec, out_shape = pl.BlockSpec((tm, N), lambda j, i: (i, j)), (M, J * N)
    else:
        out_spec, out_shape = pl.BlockSpec((None, tm, N), lambda j, i: (j, i, 0)), (J, M, N)
    return _run(
        body, (a, b), exch, name=name, grid=(J, M // tm),
        in_specs=[pl.BlockSpec((tm, K), lambda j, i: (i, 0)), pl.BlockSpec((None, K, N), lambda j, i: (j, 0, 0))],
        out_specs=out_spec, out_shape=jax.ShapeDtypeStruct(out_shape, out_dtype), sem=("parallel", "parallel"))


def mm_nn_sum(a, b, name, exch=None, tm=1024):
    J, M, K = a.shape
    N = b.shape[2]
    tm = _row_tile(M, tm)

    def body(a_ref, b_ref, o_ref):
        j = pl.program_id(1)
        p = jnp.dot(a_ref[...], b_ref[...], preferred_element_type=f32)

        @pl.when(j == 0)
        def _():
            o_ref[...] = p

        @pl.when(j > 0)
        def _():
            o_ref[...] += p

    return _run(
        body, (a, b), exch, name=name, grid=(M // tm, J),
        in_specs=[pl.BlockSpec((None, tm, K), lambda i, j: (j, i, 0)), pl.BlockSpec((None, K, N), lambda i, j: (j, 0, 0))],
        out_specs=pl.BlockSpec((tm, N), lambda i, j: (i, 0)), out_shape=jax.ShapeDtypeStruct((M, N), f32),
        sem=("parallel", "arbitrary"))


def mm_nt(a, b, out_dtype, name, tm=512):
    M, K = a.shape
    J, N, _ = b.shape
    tm = _row_tile(M, tm)

    def body(a_ref, b_ref, o_ref):
        o_ref[...] = lax.dot_general(a_ref[...], b_ref[...], NT_DIMS, preferred_element_type=f32).astype(o_ref.dtype)

    return _run(
        body, (a, b), None, name=name, grid=(J, M // tm),
        in_specs=[pl.BlockSpec((tm, K), lambda j, i: (i, 0)), pl.BlockSpec((None, N, K), lambda j, i: (j, 0, 0))],
        out_specs=pl.BlockSpec((None, tm, N), lambda j, i: (j, i, 0)), out_shape=jax.ShapeDtypeStruct((J, M, N), out_dtype),
        sem=("parallel", "parallel"))


def mm_nt_sum(a, b, name, a_cols=False, exch=None, tm=1024):
    J, N, K = b.shape
    M = a.shape[0] if a_cols else a.shape[1]
    tm = _row_tile(M, tm)

    def body(a_ref, b_ref, o_ref):
        j = pl.program_id(1)
        p = lax.dot_general(a_ref[...], b_ref[...], NT_DIMS, preferred_element_type=f32)

        @pl.when(j == 0)
        def _():
            o_ref[...] = p

        @pl.when(j > 0)
        def _():
            o_ref[...] += p

    a_spec = pl.BlockSpec((tm, K), lambda i, j: (i, j)) if a_cols else pl.BlockSpec((None, tm, K), lambda i, j: (j, i, 0))
    return _run(
        body, (a, b), exch, name=name, grid=(M // tm, J),
        in_specs=[a_spec, pl.BlockSpec((None, N, K), lambda i, j: (j, 0, 0))],
        out_specs=pl.BlockSpec((tm, N), lambda i, j: (i, 0)), out_shape=jax.ShapeDtypeStruct((M, N), f32),
        sem=("parallel", "arbitrary"))


def mm_tn(a, b, out_dtype, name, b_cols=0, exch=None, tm=1024, tn=1024, tk=2048):
    JA, T, M = a.shape
    if b_cols:
        JB, N = b_cols, b.shape[1] // b_cols
    else:
        JB, _, N = b.shape
    J = max(JA, JB)
    tm = _lane_tile(M, tm)
    tn = _lane_tile(N, tn)
    tk = _row_tile(T, tk)

    def vmem_bytes(tk):
        return 2 * 2 * tk * (tm + tn) + 2 * 4 * tm * tn + 2 * jnp.dtype(out_dtype).itemsize * tm * tn

    while vmem_bytes(tk) > MM_VMEM_BUDGET_BYTES and tk % 16 == 0:
        tk //= 2
    nk, nn = T // tk, N // tn

    def body(a_ref, b_ref, o_ref, acc):
        k = pl.program_id(3)
        p = lax.dot_general(a_ref[...], b_ref[...], TN_DIMS, preferred_element_type=f32)

        @pl.when(k == 0)
        def _():
            acc[...] = p

        @pl.when(k > 0)
        def _():
            acc[...] += p

        @pl.when(k == nk - 1)
        def _():
            o_ref[...] = acc[...].astype(o_ref.dtype)

    a_spec = pl.BlockSpec((None, tk, tm), (lambda j, i, n, k: (j, k, i)) if JA > 1 else (lambda j, i, n, k: (0, k, i)))
    if b_cols:
        b_spec = pl.BlockSpec((tk, tn), lambda j, i, n, k: (k, j * nn + n))
    else:
        b_spec = pl.BlockSpec((None, tk, tn), (lambda j, i, n, k: (j, k, n)) if JB > 1 else (lambda j, i, n, k: (0, k, n)))
    return _run(
        body, (a, b), exch, name=name, grid=(J, M // tm, nn, nk), in_specs=[a_spec, b_spec],
        out_specs=pl.BlockSpec((None, tm, tn), lambda j, i, n, k: (j, i, n)),
        out_shape=jax.ShapeDtypeStruct((J, M, N), out_dtype), scratch_shapes=[pltpu.VMEM((tm, tn), f32)],
        sem=("parallel", "parallel", "parallel", "arbitrary"))


def ln_fwd(x, u, g, b, alpha, name, tm=256):
    T, D = x.shape
    tm = _row_tile(T, tm)

    def body(x_ref, u_ref, g_ref, b_ref, y_ref, yb_ref, xh_ref, rs_ref):
        s = alpha * x_ref[...] + u_ref[...]
        mu = jnp.mean(s, axis=-1, keepdims=True)
        c = s - mu
        var = jnp.mean(c * c, axis=-1, keepdims=True)
        r = lax.rsqrt(var + LN_EPS)
        xh = c * r
        y = xh * g_ref[...] + b_ref[...]
        y_ref[...] = y
        yb_ref[...] = y.astype(bf16)
        xh_ref[...] = xh
        rs_ref[...] = r

    row = pl.BlockSpec((tm, D), lambda i: (i, 0))
    vec = pl.BlockSpec((1, D), lambda i: (0, 0))
    return _pcall(
        body, name=name, grid=(T // tm,), in_specs=[row, row, vec, vec],
        out_specs=[row, row, row, pl.BlockSpec((tm, 1), lambda i: (i, 0))],
        out_shape=[jax.ShapeDtypeStruct((T, D), f32), jax.ShapeDtypeStruct((T, D), bf16),
                   jax.ShapeDtypeStruct((T, D), f32), jax.ShapeDtypeStruct((T, 1), f32)],
        compiler_params=_cp("parallel"))(x, u, g.reshape(1, D), b.reshape(1, D))


def ln_bwd(dy, res, alpha, xhat, rstd, g, name, tm=256):
    T, D = dy.shape
    tm = _row_tile(T, tm)
    n_dy = 1 if res is None else 2

    def body(*refs):
        dy_refs, (xh_ref, rs_ref, g_ref, ds_ref, dsb_ref, dg_ref, db_ref) = refs[:n_dy], refs[n_dy:]
        i = pl.program_id(0)
        dy_t, xh = dy_refs[0][...], xh_ref[...]
        if res is not None:
            dy_t = dy_t + alpha * dy_refs[1][...]
        dxh = dy_t * g_ref[...]
        m1 = jnp.mean(dxh, axis=-1, keepdims=True)
        m2 = jnp.mean(dxh * xh, axis=-1, keepdims=True)
        ds = rs_ref[...] * (dxh - m1 - xh * m2)
        ds_ref[...] = ds
        dsb_ref[...] = ds.astype(bf16)
        pg = jnp.sum(dy_t * xh, axis=0, keepdims=True)
        pb = jnp.sum(dy_t, axis=0, keepdims=True)

        @pl.when(i == 0)
        def _():
            dg_ref[...] = pg
            db_ref[...] = pb

        @pl.when(i > 0)
        def _():
            dg_ref[...] += pg
            db_ref[...] += pb

    row = pl.BlockSpec((tm, D), lambda i: (i, 0))
    vec = pl.BlockSpec((1, D), lambda i: (0, 0))
    dys = (dy,) if res is None else (dy, res)
    return _pcall(
        body, name=name, grid=(T // tm,), in_specs=[row] * n_dy + [row, pl.BlockSpec((tm, 1), lambda i: (i, 0)), vec],
        out_specs=[row, row, vec, vec],
        out_shape=[jax.ShapeDtypeStruct((T, D), f32), jax.ShapeDtypeStruct((T, D), bf16),
                   jax.ShapeDtypeStruct((1, D), f32), jax.ShapeDtypeStruct((1, D), f32)],
        compiler_params=_cp("arbitrary"))(*dys, xhat, rstd, g.reshape(1, D))


def loss_head(y, target, name, tm=256):
    T, D = y.shape
    tm = _row_tile(T, tm)
    nt = T // tm

    def body(y_ref, t_ref, dy_ref, l_ref, acc):
        i = pl.program_id(0)
        e = y_ref[...] - t_ref[...]
        dy_ref[...] = e / D
        p = jnp.sum(e * e, axis=0, keepdims=True)

        @pl.when(i == 0)
        def _():
            acc[...] = p

        @pl.when(i > 0)
        def _():
            acc[...] += p

        @pl.when(i == nt - 1)
        def _():
            l_ref[...] = jnp.full(l_ref.shape, 0.5 * jnp.sum(acc[...]) / D, f32)

    row = pl.BlockSpec((tm, D), lambda i: (i, 0))
    return _pcall(
        body, name=name, grid=(nt,), in_specs=[row, row],
        out_specs=[row, pl.BlockSpec((1, 128), lambda i: (0, 0))],
        out_shape=[jax.ShapeDtypeStruct((T, D), f32), jax.ShapeDtypeStruct((1, 128), f32)],
        scratch_shapes=[pltpu.VMEM((1, D), f32)], compiler_params=_cp("arbitrary"))(y, target)


def pool_fwd(x, w, scale, name, tm=256):
    T, D = x.shape
    G, C, _ = w.shape
    tm = _row_tile(T, tm)
    assert all(wd & (wd - 1) == 0 and wd - 1 <= POOL_HALO for wd in POOL_WINDOWS) and tm >= POOL_HALO

    def body(x_ref, w_ref, s_ref, u_ref, p_ref, halo):
        i = pl.program_id(0)

        @pl.when(i == 0)
        def _():
            halo[...] = jnp.zeros_like(halo)

        cur = x_ref[...]
        cat = jnp.concatenate([halo[...], cur], axis=0)
        halo[...] = cur[tm - POOL_HALO:, :]
        t1 = i * tm + lax.broadcasted_iota(jnp.int32, (tm, 1), 0) + 1
        for gi, wd in enumerate(POOL_WINDOWS):
            lo, hi = gi * C, (gi + 1) * C
            win = cat[:, lo:hi]
            sh = 1
            while sh < wd:
                win = win + pltpu.roll(win, sh, 0)
                sh *= 2
            cnt = jnp.minimum(t1, wd).astype(f32)
            pooled = (win[POOL_HALO:, :] / cnt - cur[:, lo:hi]).astype(bf16)
            p_ref[:, lo:hi] = pooled
            u_ref[:, lo:hi] = jnp.dot(pooled, w_ref[gi], preferred_element_type=f32) * s_ref[:, lo:hi]

    row = pl.BlockSpec((tm, D), lambda i: (i, 0))
    return _pcall(
        body, name=name, grid=(T // tm,),
        in_specs=[row, pl.BlockSpec((G, C, C), lambda i: (0, 0, 0)), pl.BlockSpec((1, D), lambda i: (0, 0))],
        out_specs=[row, row], out_shape=[jax.ShapeDtypeStruct((T, D), f32), jax.ShapeDtypeStruct((T, D), bf16)],
        scratch_shapes=[pltpu.VMEM((POOL_HALO, D), f32)], compiler_params=_cp("arbitrary"))(x, w, scale.reshape(1, D))


def pool_bwd(du, pooled, w, scale, alpha, name, tm=256):
    T, D = du.shape
    G, C, _ = w.shape
    tm = _row_tile(T, tm)
    nt = T // tm
    n = tm + POOL_HALO

    def body(du_ref, p_ref, w_ref, s_ref, dx_ref, dw_ref, dsc_ref, halo):
        i = pl.program_id(0)

        @pl.when(i == 0)
        def _():
            halo[...] = jnp.zeros_like(halo)
            dw_ref[...] = jnp.zeros_like(dw_ref)
            dsc_ref[...] = jnp.zeros_like(dsc_ref)

        t1 = (nt - 1 - i) * tm + lax.broadcasted_iota(jnp.int32, (tm, 1), 0) + 1
        for gi, wd in enumerate(POOL_WINDOWS):
            lo, hi = gi * C, (gi + 1) * C
            du_g, pb = du_ref[:, lo:hi], p_ref[:, lo:hi]
            yg = jnp.dot(pb, w_ref[gi], preferred_element_type=f32)
            dsc_ref[:, lo:hi] += jnp.sum(du_g * yg, axis=0, keepdims=True)
            dyg = (du_g * s_ref[:, lo:hi]).astype(bf16)
            dw_ref[gi] += lax.dot_general(pb, dyg, TN_DIMS, preferred_element_type=f32)
            dp = lax.dot_general(dyg, w_ref[gi], NT_DIMS, preferred_element_type=f32)
            e = dp / jnp.minimum(t1, wd).astype(f32)
            win = jnp.concatenate([e, halo[:, lo:hi]], axis=0)
            halo[:, lo:hi] = e[:POOL_HALO, :]
            sh = 1
            while sh < wd:
                win = win + pltpu.roll(win, n - sh, 0)
                sh *= 2
            dx_ref[:, lo:hi] = alpha * du_g + win[:tm, :] - dp

    row = pl.BlockSpec((tm, D), lambda i: (nt - 1 - i, 0))
    return _pcall(
        body, name=name, grid=(nt,),
        in_specs=[row, row, pl.BlockSpec((G, C, C), lambda i: (0, 0, 0)), pl.BlockSpec((1, D), lambda i: (0, 0))],
        out_specs=[row, pl.BlockSpec((G, C, C), lambda i: (0, 0, 0)), pl.BlockSpec((1, D), lambda i: (0, 0))],
        out_shape=[jax.ShapeDtypeStruct((T, D), f32), jax.ShapeDtypeStruct((G, C, C), f32), jax.ShapeDtypeStruct((1, D), f32)],
        scratch_shapes=[pltpu.VMEM((POOL_HALO, D), f32)], compiler_params=_cp("arbitrary"))(du, pooled, w, scale.reshape(1, D))


def _sigmoid(x):
    return 0.5 * jnp.tanh(0.5 * x) + 0.5


def _causal_conv(cur, prev, w_ref, b_ref):
    cat = jnp.concatenate([prev, cur], axis=0)
    h1 = pltpu.roll(cat, 1, 0)[CONV_HALO:, :]
    h2 = pltpu.roll(cat, 2, 0)[CONV_HALO:, :]
    out = b_ref[...] + w_ref[0:1, :] * h2
    out = out + w_ref[1:2, :] * h1
    out = out + w_ref[2:3, :] * cur
    return out, h1, h2


def gate_fwd(h, cw, cb, name, tm=256):
    J2, T, FB = h.shape
    J = J2 // 2
    tm = _row_tile(T, tm)

    def body(hg_ref, hv_ref, wg_ref, wv_ref, bg_ref, bv_ref, a_ref, halo_g, halo_v):
        i = pl.program_id(1)

        @pl.when(i == 0)
        def _():
            halo_g[...] = jnp.zeros_like(halo_g)
            halo_v[...] = jnp.zeros_like(halo_v)

        hg, hv = hg_ref[...], hv_ref[...]
        gate, _, _ = _causal_conv(hg, halo_g[...], wg_ref, bg_ref)
        val, _, _ = _causal_conv(hv, halo_v[...], wv_ref, bv_ref)
        halo_g[...] = hg[tm - CONV_HALO:, :]
        halo_v[...] = hv[tm - CONV_HALO:, :]
        a_ref[...] = (gate * _sigmoid(gate) * val).astype(bf16)

    def blk(rows, off):
        return pl.BlockSpec((None, rows, FB), lambda j, i: (j + off, i if rows == tm else 0, 0))

    return _pcall(
        body, name=name, grid=(J, T // tm),
        in_specs=[blk(tm, 0), blk(tm, J), blk(3, 0), blk(3, J), blk(1, 0), blk(1, J)],
        out_specs=pl.BlockSpec((None, tm, FB), lambda j, i: (j, i, 0)), out_shape=jax.ShapeDtypeStruct((J, T, FB), bf16),
        scratch_shapes=[pltpu.VMEM((CONV_HALO, FB), f32), pltpu.VMEM((CONV_HALO, FB), f32)],
        compiler_params=_cp("parallel", "arbitrary"))(h, h, cw, cw, cb, cb)


def gate_bwd(h, da, cw, cb, name, tm=256):
    J2, T, FB = h.shape
    J = J2 // 2
    tm = _row_tile(T, tm)
    nt = T // tm
    n = tm + CONV_HALO

    def body(hg_ref, hv_ref, pg_ref, pv_ref, da_ref, wg_ref, wv_ref, bg_ref, bv_ref,
             dhg_ref, dhv_ref, dwg_ref, dwv_ref, dbg_ref, dbv_ref, halo_g, halo_v):
        i = pl.program_id(1)

        @pl.when(i == 0)
        def _():
            for r in (halo_g, halo_v, dwg_ref, dwv_ref, dbg_ref, dbv_ref):
                r[...] = jnp.zeros_like(r)

        has_prev = (i < nt - 1).astype(f32)
        hg, hv = hg_ref[...], hv_ref[...]
        gate, g1, g2 = _causal_conv(hg, pg_ref[...] * has_prev, wg_ref, bg_ref)
        val, v1, v2 = _causal_conv(hv, pv_ref[...] * has_prev, wv_ref, bv_ref)
        sg = _sigmoid(gate)
        da_t = da_ref[...]
        dval = da_t * (gate * sg)
        dgate = da_t * val * (sg * (1.0 + gate * (1.0 - sg)))

        def back(d, cur, s1, s2, w_ref, dw_ref, db_ref, halo, dh_ref):
            db_ref[...] += jnp.sum(d, axis=0, keepdims=True)
            dw_ref[0:1, :] += jnp.sum(d * s2, axis=0, keepdims=True)
            dw_ref[1:2, :] += jnp.sum(d * s1, axis=0, keepdims=True)
            dw_ref[2:3, :] += jnp.sum(d * cur, axis=0, keepdims=True)
            cat = jnp.concatenate([d, halo[...]], axis=0)
            halo[...] = d[:CONV_HALO, :]
            d1 = pltpu.roll(cat, n - 1, 0)[:tm, :]
            d2 = pltpu.roll(cat, n - 2, 0)[:tm, :]
            dh = w_ref[2:3, :] * d + w_ref[1:2, :] * d1 + w_ref[0:1, :] * d2
            dh_ref[...] = dh.astype(bf16)

        back(dgate, hg, g1, g2, wg_ref, dwg_ref, dbg_ref, halo_g, dhg_ref)
        back(dval, hv, v1, v2, wv_ref, dwv_ref, dbv_ref, halo_v, dhv_ref)

    def rows(off):
        return pl.BlockSpec((None, tm, FB), lambda j, i: (j + off, nt - 1 - i, 0))

    def prev(off):
        return pl.BlockSpec((None, CONV_HALO, FB), lambda j, i: (j + off, jnp.maximum((nt - 1 - i) * (tm // CONV_HALO) - 1, 0), 0))

    def small(r, off):
        return pl.BlockSpec((None, r, FB), lambda j, i: (j + off, 0, 0))

    dhg, dhv, dwg, dwv, dbg, dbv = _pcall(
        body, name=name, grid=(J, nt),
        in_specs=[rows(0), rows(J), prev(0), prev(J), rows(0), small(3, 0), small(3, J), small(1, 0), small(1, J)],
        out_specs=[rows(0), rows(0), small(3, 0), small(3, 0), small(1, 0), small(1, 0)],
        out_shape=[jax.ShapeDtypeStruct((J, T, FB), bf16), jax.ShapeDtypeStruct((J, T, FB), bf16),
                   jax.ShapeDtypeStruct((J, 3, FB), f32), jax.ShapeDtypeStruct((J, 3, FB), f32),
                   jax.ShapeDtypeStruct((J, 1, FB), f32), jax.ShapeDtypeStruct((J, 1, FB), f32)],
        scratch_shapes=[pltpu.VMEM((CONV_HALO, FB), f32), pltpu.VMEM((CONV_HALO, FB), f32)],
        compiler_params=_cp("parallel", "arbitrary"))(h, h, h, h, da, cw, cw, cb, cb)
    return (jnp.concatenate([dhg, dhv], axis=0), jnp.concatenate([dwg, dwv], axis=0), jnp.concatenate([dbg, dbv], axis=0))


def _split_dot(x, tri):
    hi = x.astype(bf16)
    lo = (x - hi.astype(f32)).astype(bf16)
    return jnp.dot(hi, tri, preferred_element_type=f32) + jnp.dot(lo, tri, preferred_element_type=f32)


def _attn_block(q, kblk, k0, q_pos, valid, carry_r, tri_incl, scale):
    tq, tk = q.shape[0], kblk.shape[0]
    z = lax.dot_general(q, kblk, NT_DIMS, preferred_element_type=f32) * scale
    mask = jnp.logical_and((k0 + lax.broadcasted_iota(jnp.int32, (tq, tk), 1)) < q_pos, valid)
    e = jnp.exp(-jnp.abs(z))
    lsm = jnp.where(mask, -(jnp.maximum(z, 0.0) + jnp.log(1.0 + e)), 0.0)
    rest = _split_dot(lsm, tri_incl) + carry_r
    a = jnp.where(mask, jnp.exp(z + rest), 0.0)
    return mask, z, e, lsm, a


def _tri(tk, strict):
    r = lax.broadcasted_iota(jnp.int32, (tk, tk), 0)
    c = lax.broadcasted_iota(jnp.int32, (tk, tk), 1)
    return ((r > c) if strict else (r >= c)).astype(bf16)


def _attn_geometry(T, tq, tk, nsub):
    tq = _row_tile(T, tq)
    nsub = max(1, min(nsub, T // tq))
    while T % (nsub * tq):
        nsub -= 1
    return tq, min(tk, tq), nsub


def _chain_block(it, nkb, tk):
    return it < nkb, pl.multiple_of(jnp.maximum(nkb - 1 - it, 0) * tk, tk)


def _sweep_live(it, rs, nkb):
    m = None
    for r, n in zip(rs, nkb):
        ra = jnp.where(it < n, r, 2.0 * EXP_UNDERFLOW)
        m = ra if m is None else jnp.maximum(m, ra)
    return jnp.max(m) > EXP_UNDERFLOW


def attn_fwd(qkv, name, exch=None, tq=256, tk=128, nsub=4):
    T, D3 = qkv.shape
    D = D3 // 3
    H = D // HEAD_DIM
    tq, tk, nsub = _attn_geometry(T, tq, tk, nsub)
    bq = nsub * tq
    scale = HEAD_DIM ** -0.5

    def body(q_ref, k_ref, v_ref, o_ref):
        i = pl.program_id(1)
        tri = _tri(tk, False)
        row = lax.broadcasted_iota(jnp.int32, (tq, tk), 0)
        nkb = [(i * nsub + a + 1) * (tq // tk) for a in range(nsub)]
        q_pos = [(i * nsub + a) * tq + row for a in range(nsub)]
        qs = [q_ref[a * tq:(a + 1) * tq, :] for a in range(nsub)]

        def cond(c):
            return _sweep_live(c[0], c[1], nkb)

        def step(c):
            it, rs, accs = c
            rs2, accs2 = [], []
            for a in range(nsub):
                valid, k0 = _chain_block(it, nkb[a], tk)
                _, _, _, lsm, w = _attn_block(qs[a], k_ref[pl.ds(k0, tk), :], k0, q_pos[a], valid, rs[a], tri, scale)
                accs2.append(accs[a] + jnp.dot(w.astype(bf16), v_ref[pl.ds(k0, tk), :], preferred_element_type=f32))
                rs2.append(rs[a] + jnp.sum(lsm, axis=-1, keepdims=True))
            return it + 1, rs2, accs2

        init = (jnp.int32(0), [jnp.zeros((tq, 1), f32)] * nsub, [jnp.zeros((tq, HEAD_DIM), f32)] * nsub)
        _, _, accs = lax.while_loop(cond, step, init)
        for a in range(nsub):
            o_ref[a * tq:(a + 1) * tq, :] = accs[a].astype(bf16)

    return _run(
        body, (qkv, qkv, qkv), exch, name=name, grid=(H, T // bq),
        in_specs=[pl.BlockSpec((bq, HEAD_DIM), lambda h, i: (i, h)),
                  pl.BlockSpec((T, HEAD_DIM), lambda h, i: (0, H + h)),
                  pl.BlockSpec((T, HEAD_DIM), lambda h, i: (0, 2 * H + h))],
        out_specs=pl.BlockSpec((bq, HEAD_DIM), lambda h, i: (i, h)), out_shape=jax.ShapeDtypeStruct((T, D), bf16),
        sem=("parallel", "arbitrary"))


def attn_bwd(qkv, do, name, exch=None, tq=256, tk=128, nsub=4):
    T, D3 = qkv.shape
    D = D3 // 3
    H = D // HEAD_DIM
    tq, tk, nsub = _attn_geometry(T, tq, tk, nsub)
    bq = nsub * tq
    nq = T // bq
    scale = HEAD_DIM ** -0.5

    def body(q_ref, k_ref, v_ref, do_ref, dq_ref, dk_ref, dv_ref, dk_acc, dv_acc):
        i = pl.program_id(1)

        @pl.when(i == 0)
        def _():
            dk_acc[...] = jnp.zeros_like(dk_acc)
            dv_acc[...] = jnp.zeros_like(dv_acc)

        tri_incl, tri_excl = _tri(tk, False), _tri(tk, True)
        row = lax.broadcasted_iota(jnp.int32, (tq, tk), 0)
        nkb = [(i * nsub + a + 1) * (tq // tk) for a in range(nsub)]
        q_pos = [(i * nsub + a) * tq + row for a in range(nsub)]
        qs = [q_ref[a * tq:(a + 1) * tq, :] for a in range(nsub)]
        dos = [do_ref[a * tq:(a + 1) * tq, :] for a in range(nsub)]
        zeros = [jnp.zeros((tq, 1), f32)] * nsub

        def cond(c):
            return _sweep_live(c[0], c[1], nkb)

        def weights(a, it, r):
            valid, k0 = _chain_block(it, nkb[a], tk)
            mask, z, e, lsm, w = _attn_block(qs[a], k_ref[pl.ds(k0, tk), :], k0, q_pos[a], valid, r, tri_incl, scale)
            g = w * lax.dot_general(dos[a], v_ref[pl.ds(k0, tk), :], NT_DIMS, preferred_element_type=f32)
            return k0, mask, z, e, lsm, w, g

        def total_step(c):
            it, rs, tots = c
            rs2, tots2 = [], []
            for a in range(nsub):
                _, _, _, _, lsm, _, g = weights(a, it, rs[a])
                rs2.append(rs[a] + jnp.sum(lsm, axis=-1, keepdims=True))
                tots2.append(tots[a] + jnp.sum(g, axis=-1, keepdims=True))
            return it + 1, rs2, tots2

        _, _, g_total = lax.while_loop(cond, total_step, (jnp.int32(0), zeros, zeros))

        def grad_step(c):
            it, rs, runs, dqs = c
            rs2, runs2, dqs2 = [], [], []
            for a in range(nsub):
                k0, mask, z, e, lsm, w, g = weights(a, it, rs[a])
                prefix = g_total[a] - (runs[a] + _split_dot(g, tri_excl))
                sig = jnp.where(z >= 0.0, 1.0, e) / (1.0 + e)
                dz = (jnp.where(mask, g - sig * prefix, 0.0) * scale).astype(bf16)
                dqs2.append(dqs[a] + jnp.dot(dz, k_ref[pl.ds(k0, tk), :], preferred_element_type=f32))
                dk_acc[pl.ds(k0, tk), :] += lax.dot_general(dz, qs[a], TN_DIMS, preferred_element_type=f32)
                dv_acc[pl.ds(k0, tk), :] += lax.dot_general(w.astype(bf16), dos[a], TN_DIMS, preferred_element_type=f32)
                rs2.append(rs[a] + jnp.sum(lsm, axis=-1, keepdims=True))
                runs2.append(runs[a] + jnp.sum(g, axis=-1, keepdims=True))
            return it + 1, rs2, runs2, dqs2

        init = (jnp.int32(0), zeros, zeros, [jnp.zeros((tq, HEAD_DIM), f32)] * nsub)
        _, _, _, dqs = lax.while_loop(cond, grad_step, init)
        for a in range(nsub):
            dq_ref[a * tq:(a + 1) * tq, :] = dqs[a].astype(bf16)

        @pl.when(i == nq - 1)
        def _():
            dk_ref[...] = dk_acc[...].astype(bf16)
            dv_ref[...] = dv_acc[...].astype(bf16)

    qblk = pl.BlockSpec((bq, HEAD_DIM), lambda h, i: (i, h))
    head = pl.BlockSpec((T, HEAD_DIM), lambda h, i: (0, h))
    return _run(
        body, (qkv, qkv, qkv, do), exch, name=name, grid=(H, nq),
        in_specs=[qblk, pl.BlockSpec((T, HEAD_DIM), lambda h, i: (0, H + h)),
                  pl.BlockSpec((T, HEAD_DIM), lambda h, i: (0, 2 * H + h)), qblk],
        out_specs=[qblk, head, head], out_shape=[jax.ShapeDtypeStruct((T, D), bf16)] * 3,
        scratch_shapes=[pltpu.VMEM((T, HEAD_DIM), f32), pltpu.VMEM((T, HEAD_DIM), f32)],
        sem=("parallel", "arbitrary"))


def sum_partials(p, name, tr=256):
    K, R, C = p.shape
    tr = _row_tile(R, tr)

    def body(p_ref, o_ref):
        g = p_ref[0].astype(f32)
        for k in range(1, K):
            g = g + p_ref[k].astype(f32)
        o_ref[...] = g

    return _pcall(
        body, name=name, grid=(R // tr,), in_specs=[pl.BlockSpec((K, tr, C), lambda i: (0, i, 0))],
        out_specs=pl.BlockSpec((tr, C), lambda i: (i, 0)), out_shape=jax.ShapeDtypeStruct((R, C), f32),
        compiler_params=_cp("parallel"))(p)


def adamw(p, w, m, v, name, tr=256):
    K, R, C = p.shape
    tr = _row_tile(R, tr)
    c1 = 1.0 - ADAM_B1 ** ADAM_STEP
    c2 = 1.0 - ADAM_B2 ** ADAM_STEP

    def body(p_ref, w_ref, m_ref, v_ref, g_ref, d_ref, nm_ref, nv_ref):
        g = p_ref[0].astype(f32)
        for k in range(1, K):
            g = g + p_ref[k].astype(f32)
        nm = ADAM_B1 * m_ref[...] + (1.0 - ADAM_B1) * g
        nv = ADAM_B2 * v_ref[...] + (1.0 - ADAM_B2) * (g * g)
        g_ref[...] = g
        nm_ref[...] = nm
        nv_ref[...] = nv
        d_ref[...] = -ADAM_LR * ((nm / c1) / (jnp.sqrt(nv / c2) + ADAM_EPS) + ADAM_WD * w_ref[...])

    row = pl.BlockSpec((tr, C), lambda i: (i, 0))
    return _pcall(
        body, name=name, grid=(R // tr,), in_specs=[pl.BlockSpec((K, tr, C), lambda i: (0, i, 0)), row, row, row],
        out_specs=[row] * 4, out_shape=[jax.ShapeDtypeStruct((R, C), f32)] * 4, compiler_params=_cp("parallel"))(p, w, m, v)


def _pack(parts):
    flat, slices, off = [], [], 0
    for a in parts:
        nel = a.size
        pad = -nel % PACK_ALIGN
        flat.append(jnp.pad(a.reshape(-1).astype(f32), (0, pad)))
        slices.append((off, nel, a.shape))
        off += nel + pad
    return jnp.concatenate(flat).reshape(-1, 128), slices


def _unpack(packed, slices):
    flat = packed.reshape(-1)
    return [flat[off:off + nel].reshape(shape) for off, nel, shape in slices]


def kernel(x, pool_w, pool_scale, attn_w_qkv, attn_w_o, ffn_w_up, ffn_conv_w, ffn_conv_b, ffn_w_down, ln_mix_g, ln_mix_b, ln_ffn_g, ln_ffn_b, loss_target, m_pool_w, m_pool_scale, m_attn_w_qkv, m_attn_w_o, m_ffn_w_up, m_ffn_conv_w, m_ffn_conv_b, m_ffn_w_down, m_ln_mix_g, m_ln_mix_b, m_ln_ffn_g, m_ln_ffn_b, v_pool_w, v_pool_scale, v_attn_w_qkv, v_attn_w_o, v_ffn_w_up, v_ffn_conv_w, v_ffn_conv_b, v_ffn_w_down, v_ln_mix_g, v_ln_mix_b, v_ln_ffn_g, v_ln_ffn_b):
    _, T, D = x.shape
    depth = ln_mix_g.shape[0]
    alpha = (2.0 * depth) ** 0.25
    G, CS, C = pool_w.shape[1:]
    FB = ffn_w_up.shape[2]
    JH = N_DEV // 2
    me = 4 * lax.axis_index("x") + 2 * lax.axis_index("y") + lax.axis_index("c")

    shards = {}
    for l in range(depth):
        j = l // 2
        shards["up", l], shards["down", l] = ffn_w_up[l].astype(bf16), ffn_w_down[l].astype(bf16)
        if l % 2 == 0:
            shards["pool", l] = pool_w[j].astype(bf16)
        else:
            shards["qkv", l], shards["wo", l] = attn_w_qkv[j].astype(bf16), attn_w_o[j].astype(bf16)
    full = {}

    def gather_in(call, keys):
        keys = [k for k in keys if k[1] < depth]
        if not keys:
            return call(None)
        res, got = call(([shards[k] for k in keys], False))
        full.update(zip(keys, got))
        return res

    first = [("pool", 0), ("up", 0)]
    got = all_gather([shards[k] for k in first] + [ffn_conv_w], "gather_first")
    full.update(zip(first, got))
    cw_all = got[-1]

    xf = x[0]
    xb = xf.astype(bf16)
    saved, weights = [], []
    for l in range(depth):
        j, sv, wl = l // 2, {}, {}
        if l % 2 == 0:
            wl["pool"] = full["pool", l].transpose(1, 0, 2, 3).reshape(G, C, C)
            u, sv["pooled"] = pool_fwd(xf, wl["pool"], pool_scale[j], f"pool_fwd_l{l}")
            up_rides = ([("down", l)] if l == 0 else [("qkv", l + 1)]) + [("wo", l + 1)]
            down_rides = [("qkv", l + 1)] if l == 0 else []
        else:
            wl["qkv"], wl["wo"] = full["qkv", l], full["wo", l].reshape(1, D, D)
            sv["xin_b"] = xb
            sv["qkv"] = mm_nn(xb, wl["qkv"], bf16, f"qkv_l{l}", out_cols=True)
            sv["o"] = gather_in(lambda e: attn_fwd(sv["qkv"], f"attn_fwd_l{l}", exch=e),
                                [("up", l), ("down", l), ("pool", l + 1)])
            u = mm_nn(sv["o"], wl["wo"], f32, f"attn_out_l{l}")[0]
            up_rides, down_rides = [("up", l + 1)], [("down", l + 1)]
        x1, sv["x1_b"], sv["xhat1"], sv["rstd1"] = ln_fwd(xf, u, ln_mix_g[l], ln_mix_b[l], alpha, f"ln_mix_l{l}")
        wl["up"] = full["up", l]
        sv["h"] = gather_in(lambda e: mm_nn(sv["x1_b"], wl["up"], f32, f"ffn_up_l{l}", exch=e), up_rides)
        wl["cw"], wl["cb"] = cw_all[:, l], ffn_conv_b[l].reshape(N_DEV, 1, FB)
        sv["a"] = gate_fwd(sv["h"], wl["cw"], wl["cb"], f"ffn_gate_l{l}")
        wl["down"] = full["down", l].reshape(JH, FB, D)
        f = gather_in(lambda e: mm_nn_sum(sv["a"], wl["down"], f"ffn_down_l{l}", exch=e), down_rides)
        xf, xb, sv["xhat2"], sv["rstd2"] = ln_fwd(x1, f, ln_ffn_g[l], ln_ffn_b[l], alpha, f"ln_ffn_l{l}")
        saved.append(sv)
        weights.append(wl)

    dx, loss_row = loss_head(xf, loss_target[0], "loss_head")

    landed = {}

    def scatter_in(call, parts):
        res, got = call(([p for _, p in parts], True))
        landed.update(zip([k for k, _ in parts], got))
        return res

    grads = [None] * depth
    res, pending = None, []
    for l in reversed(range(depth)):
        wl, sv, j, gl = weights[l], saved[l], l // 2, {}
        ds2, ds2_b, gl["ln_ffn_g"], gl["ln_ffn_b"] = ln_bwd(dx, res, alpha, sv["xhat2"], sv["rstd2"], ln_ffn_g[l], f"ln_ffn_bwd_l{l}")
        da = mm_nt(ds2_b, wl["down"], f32, f"ffn_down_dx_l{l}")
        g_down = mm_tn(sv["a"], ds2_b[None], bf16, f"ffn_down_dw_l{l}").reshape(N_DEV, FB // 2, D)
        dh, gl["cw"], gl["cb"] = gate_bwd(sv["h"], da, wl["cw"], wl["cb"], f"ffn_gate_bwd_l{l}")
        g_up = scatter_in(lambda e: mm_tn(sv["x1_b"][None], dh, bf16, f"ffn_up_dw_l{l}", exch=e),
                          [(("down", l), g_down)] + pending)
        dx1 = scatter_in(lambda e: mm_nt_sum(dh, wl["up"], f"ffn_up_dx_l{l}", exch=e), [(("up", l), g_up)])
        ds1, ds1_b, gl["ln_mix_g"], gl["ln_mix_b"] = ln_bwd(dx1, ds2, alpha, sv["xhat1"], sv["rstd1"], ln_mix_g[l], f"ln_mix_bwd_l{l}")
        if l % 2 == 0:
            dx, g_pool, gl["pool_scale"] = pool_bwd(ds1, sv["pooled"], wl["pool"], pool_scale[j], alpha, f"pool_bwd_l{l}")
            res, pending = None, [(("pool", l), g_pool.reshape(G, N_DEV, CS, C).transpose(1, 0, 2, 3).astype(bf16))]
        else:
            do = mm_nt(ds1_b, wl["wo"], bf16, f"attn_out_dx_l{l}")[0]
            g_wo = mm_tn(sv["o"][None], ds1_b[None], bf16, f"attn_out_dw_l{l}").reshape(N_DEV, D // N_DEV, D)
            dqkv = jnp.concatenate(scatter_in(lambda e: attn_bwd(sv["qkv"], do, f"attn_bwd_l{l}", exch=e),
                                              [(("wo", l), g_wo)]), axis=1)
            g_qkv = mm_tn(sv["xin_b"][None], dqkv, bf16, f"qkv_dw_l{l}", b_cols=N_DEV)
            dx = scatter_in(lambda e: mm_nt_sum(dqkv, wl["qkv"], f"qkv_dx_l{l}", a_cols=True, exch=e), [(("qkv", l), g_qkv)])
            res, pending = ds1, []
        grads[l] = gl
    assert res is None, "the first layer is a pooling layer: its backward adds the residual path itself"
    grad_x = dx[None]
    if pending:
        landed.update(zip([k for k, _ in pending], scatter_partials([p for _, p in pending], "scatter_last")))

    def stack_update(name, kind, w, m, v):
        shard = w.shape[1:]
        rows = w[0].size // shard[-1]
        layers = sorted(l for k, l in landed if k == kind)
        outs = [adamw(landed[kind, l].reshape(N_DEV, rows, shard[-1]), w[i].reshape(rows, -1), m[i].reshape(rows, -1),
                      v[i].reshape(rows, -1), f"adamw_{name}_{i}") for i, l in enumerate(layers)]
        return [jnp.stack([o[t].reshape(shard) for o in outs]) for t in range(4)]

    big = {
        "pool_w": stack_update("pool_w", "pool", pool_w, m_pool_w, v_pool_w),
        "attn_w_qkv": stack_update("attn_w_qkv", "qkv", attn_w_qkv, m_attn_w_qkv, v_attn_w_qkv),
        "attn_w_o": stack_update("attn_w_o", "wo", attn_w_o, m_attn_w_o, v_attn_w_o),
        "ffn_w_up": stack_update("ffn_w_up", "up", ffn_w_up, m_ffn_w_up, v_ffn_w_up),
        "ffn_w_down": stack_update("ffn_w_down", "down", ffn_w_down, m_ffn_w_down, v_ffn_w_down),
    }

    def per_layer(key):
        return jnp.stack([grads[l][key].reshape(-1) for l in range(depth)])

    small_parts = [per_layer("ln_mix_g"), per_layer("ln_mix_b"), per_layer("ln_ffn_g"), per_layer("ln_ffn_b"),
                   jnp.stack([grads[l]["pool_scale"].reshape(-1) for l in range(0, depth, 2)]),
                   per_layer("cb"), jnp.stack([grads[l]["cw"] for l in range(depth)]), loss_row[0, :1]]
    packed, slices = _pack(small_parts)
    total = sum_partials(all_gather([packed], "gather_small")[0], "sum_small")
    g_mix_g, g_mix_b, g_ffn_g, g_ffn_b, g_scale, g_cb, g_cw_all, loss = _unpack(total, slices)
    g_cw = lax.dynamic_index_in_dim(g_cw_all, me, axis=1, keepdims=False)

    small_w = [ln_mix_g, ln_mix_b, ln_ffn_g, ln_ffn_b, pool_scale, ffn_conv_b, ffn_conv_w]
    small_m = [m_ln_mix_g, m_ln_mix_b, m_ln_ffn_g, m_ln_ffn_b, m_pool_scale, m_ffn_conv_b, m_ffn_conv_w]
    small_v = [v_ln_mix_g, v_ln_mix_b, v_ln_ffn_g, v_ln_ffn_b, v_pool_scale, v_ffn_conv_b, v_ffn_conv_w]
    small_g = [g_mix_g, g_mix_b, g_ffn_g, g_ffn_b, g_scale, g_cb.reshape(ffn_conv_b.shape), g_cw]
    pw, wslices = _pack(small_w)
    res = adamw(_pack(small_g)[0][None], pw, _pack(small_m)[0], _pack(small_v)[0], "adamw_small")
    small = dict(zip(["ln_mix_g", "ln_mix_b", "ln_ffn_g", "ln_ffn_b", "pool_scale", "ffn_conv_b", "ffn_conv_w"],
                     zip(*[_unpack(r, wslices) for r in res])))

    order = ["pool_w", "pool_scale", "attn_w_qkv", "attn_w_o", "ffn_w_up", "ffn_conv_w", "ffn_conv_b", "ffn_w_down",
             "ln_mix_g", "ln_mix_b", "ln_ffn_g", "ln_ffn_b"]
    table = {**big, **{k: list(val) for k, val in small.items()}}
    outs = [loss.reshape(()), grad_x]
    for t in range(4):
        outs += [table[name][t] for name in order]
    return tuple(outs)
```

```python
import jax
import jax.numpy as jnp
from jax import lax
from jax.experimental import pallas as pl
from jax.experimental.pallas import tpu as pltpu

f32, bf16 = jnp.float32, jnp.bfloat16

N_DEV = 8
HEAD_DIM = 128
POOL_WINDOWS = (2, 4, 8, 16)
POOL_HALO = 16
CONV_HALO = 8
LN_EPS = 1e-5
ADAM_LR, ADAM_B1, ADAM_B2, ADAM_EPS, ADAM_WD, ADAM_STEP = 0.001, 0.9, 0.999, 1e-08, 0.01, 10
V7X_VMEM_BYTES = 64 * 1024 * 1024
VMEM_LIMIT_BYTES = V7X_VMEM_BYTES - 8 * 1024 * 1024
MM_VMEM_BUDGET_BYTES = 40 * 1024 * 1024
EXP_UNDERFLOW = -104.0
PACK_ALIGN = 8 * 128

NT_DIMS = (((1,), (1,)), ((), ()))
TN_DIMS = (((0,), (0,)), ((), ()))


def _pcall(body, **kw):
    return pl.pallas_call(body, **kw)


def _cp(*sem):
    return pltpu.CompilerParams(dimension_semantics=sem, vmem_limit_bytes=VMEM_LIMIT_BYTES)


def _row_tile(rows, cap):
    if rows <= cap:
        return rows
    best = None
    for t in range(8, cap + 1, 8):
        if rows % t == 0:
            best = t
    assert best is not None, rows
    return best


def _lane_tile(cols, cap):
    best = cols
    for t in range(128, min(cap, cols) + 1, 128):
        if cols % t == 0:
            best = t
    return best if cols > cap else cols


N_PEER = N_DEV - 1


def _mesh_place():
    x, y, c = lax.axis_index("x"), lax.axis_index("y"), lax.axis_index("c")
    peers = []
    for k in range(1, N_DEV):
        px = 1 - x if k & 4 else x
        py = 1 - y if k & 2 else y
        pc = 1 - c if k & 1 else c
        peers.append(((px, py, pc), 4 * px + 2 * py + pc))
    return 4 * x + 2 * y + c, peers


def _exchange_copies(ins, outs, send, recv, loc, place, scatter):
    me, peers = place

    def src(a, idx):
        return ins[a].at[idx] if scatter else ins[a]

    local, starts, waits = [], [], []
    for a in range(len(ins)):
        local.append(pltpu.make_async_copy(src(a, me), outs[a].at[me], loc.at[a]))
        for k, (dev, idx) in enumerate(peers):
            sems = dict(send_sem=send.at[a * N_PEER + k], recv_sem=recv.at[a * N_PEER + k],
                        device_id=dev, device_id_type=pl.DeviceIdType.MESH)
            starts.append(pltpu.make_async_remote_copy(src_ref=src(a, idx), dst_ref=outs[a].at[me], **sems))
            waits.append(pltpu.make_async_remote_copy(src_ref=src(a, idx), dst_ref=outs[a].at[idx], **sems))
    return local, starts, waits


def _exchange_shapes(arrs, scatter):
    return [jax.ShapeDtypeStruct(a.shape if scatter else (N_DEV, *a.shape), a.dtype) for a in arrs]


def _exchange_sems(n):
    return [pltpu.SemaphoreType.DMA((n * N_PEER,)), pltpu.SemaphoreType.DMA((n * N_PEER,)), pltpu.SemaphoreType.DMA((n,))]


def _exchange(arrs, scatter, name):
    n = len(arrs)

    def body(*refs):
        local, starts, waits = _exchange_copies(refs[:n], refs[n:2 * n], *refs[2 * n:], _mesh_place(), scatter)
        for cp in local + starts:
            cp.start()
        for cp in waits + local:
            cp.wait()

    any_spec = pl.BlockSpec(memory_space=pl.ANY)
    return _pcall(body, name=name, in_specs=[any_spec] * n, out_specs=[any_spec] * n,
                  out_shape=_exchange_shapes(arrs, scatter), scratch_shapes=_exchange_sems(n))(*arrs)


def all_gather(shards, name):
    return _exchange(shards, False, name)


def scatter_partials(partials, name):
    return _exchange(partials, True, name)


def _run(body, args, exch, *, name, grid, in_specs, out_specs, out_shape, scratch_shapes=(), sem):
    scratch_shapes = list(scratch_shapes)
    if exch is None:
        return _pcall(body, name=name, grid=grid, in_specs=in_specs, out_specs=out_specs, out_shape=out_shape,
                      scratch_shapes=scratch_shapes, compiler_params=_cp(*sem))(*args)
    arrs, scatter = exch
    single = not isinstance(out_shape, (list, tuple))
    out_specs, out_shape = ([out_specs], [out_shape]) if single else (list(out_specs), list(out_shape))
    n, na, nb, ns = len(arrs), len(in_specs), len(out_specs), len(scratch_shapes)

    def hosted(*refs):
        ins, xin = refs[:na], refs[na:na + n]
        outs, xout = refs[na + n:na + n + nb], refs[na + n + nb:na + 2 * n + nb]
        scr, sems = refs[na + 2 * n + nb:na + 2 * n + nb + ns], refs[na + 2 * n + nb + ns:]
        first, last = None, None
        for d, size in enumerate(grid):
            pid = pl.program_id(d)
            first = (pid == 0) if first is None else jnp.logical_and(first, pid == 0)
            last = (pid == size - 1) if last is None else jnp.logical_and(last, pid == size - 1)
        local, starts, waits = _exchange_copies(xin, xout, *sems, _mesh_place(), scatter)

        @pl.when(first)
        def _():
            for cp in local + starts:
                cp.start()

        body(*ins, *outs, *scr)

        @pl.when(last)
        def _():
            for cp in waits + local:
                cp.wait()

    any_spec = pl.BlockSpec(memory_space=pl.ANY)
    res = _pcall(hosted, name=name, grid=grid, in_specs=list(in_specs) + [any_spec] * n,
                 out_specs=out_specs + [any_spec] * n, out_shape=out_shape + _exchange_shapes(arrs, scatter),
                 scratch_shapes=scratch_shapes + _exchange_sems(n),
                 compiler_params=_cp(*(["arbitrary"] * len(grid))))(*args, *arrs)
    return (res[0] if single else list(res[:nb])), list(res[nb:])


def mm_nn(a, b, out_dtype, name, out_cols=False, exch=None, tm=512):
    M, K = a.shape
    J, _, N = b.shape
    tm = _row_tile(M, tm)

    def body(a_ref, b_ref, o_ref):
        o_ref[...] = jnp.dot(a_ref[...], b_ref[...], preferred_element_type=f32).astype(o_ref.dtype)

    if out_cols:
        out_spec, out_shape = pl.BlockSpec((tm, N), lambda j, i: (i, j)), (M, J * N)
    else:
        out_spec, out_shape = pl.BlockSpec((None, tm, N), lambda j, i: (j, i, 0)), (J, M, N)
    return _run(
        body, (a, b), exch, name=name, grid=(J, M // tm),
        in_specs=[pl.BlockSpec((tm, K), lambda j, i: (i, 0)), pl.BlockSpec((None, K, N), lambda j, i: (j, 0, 0))],
        out_specs=out_spec, out_shape=jax.ShapeDtypeStruct(out_shape, out_dtype), sem=("parallel", "parallel"))


def mm_nn_sum(a, b, name, exch=None, tm=1024):
    J, M, K = a.shape
    N = b.shape[2]
    tm = _row_tile(M, tm)

    def body(a_ref, b_ref, o_ref):
        j = pl.program_id(1)
        p = jnp.dot(a_ref[...], b_ref[...], preferred_element_type=f32)

        @pl.when(j == 0)
        def _():
            o_ref[...] = p

        @pl.when(j > 0)
        def _():
            o_ref[...] += p

    return _run(
        body, (a, b), exch, name=name, grid=(M // tm, J),
        in_specs=[pl.BlockSpec((None, tm, K), lambda i, j: (j, i, 0)), pl.BlockSpec((None, K, N), lambda i, j: (j, 0, 0))],
        out_specs=pl.BlockSpec((tm, N), lambda i, j: (i, 0)), out_shape=jax.ShapeDtypeStruct((M, N), f32),
        sem=("parallel", "arbitrary"))


def mm_nt(a, b, out_dtype, name, exch=None, tm=512):
    M, K = a.shape
    J, N, _ = b.shape
    tm = _row_tile(M, tm)

    def body(a_ref, b_ref, o_ref):
        o_ref[...] = lax.dot_general(a_ref[...], b_ref[...], NT_DIMS, preferred_element_type=f32).astype(o_ref.dtype)

    return _run(
        body, (a, b), exch, name=name, grid=(J, M // tm),
        in_specs=[pl.BlockSpec((tm, K), lambda j, i: (i, 0)), pl.BlockSpec((None, N, K), lambda j, i: (j, 0, 0))],
        out_specs=pl.BlockSpec((None, tm, N), lambda j, i: (j, i, 0)), out_shape=jax.ShapeDtypeStruct((J, M, N), out_dtype),
        sem=("parallel", "parallel"))


def mm_nt_sum(a, b, name, a_cols=False, exch=None, tm=1024):
    J, N, K = b.shape
    M = a.shape[0] if a_cols else a.shape[1]
    tm = _row_tile(M, tm)

    def body(a_ref, b_ref, o_ref):
        j = pl.program_id(1)
        p = lax.dot_general(a_ref[...], b_ref[...], NT_DIMS, preferred_element_type=f32)

        @pl.when(j == 0)
        def _():
            o_ref[...] = p

        @pl.when(j > 0)
        def _():
            o_ref[...] += p

    a_spec = pl.BlockSpec((tm, K), lambda i, j: (i, j)) if a_cols else pl.BlockSpec((None, tm, K), lambda i, j: (j, i, 0))
    return _run(
        body, (a, b), exch, name=name, grid=(M // tm, J),
        in_specs=[a_spec, pl.BlockSpec((None, N, K), lambda i, j: (j, 0, 0))],
        out_specs=pl.BlockSpec((tm, N), lambda i, j: (i, 0)), out_shape=jax.ShapeDtypeStruct((M, N), f32),
        sem=("parallel", "arbitrary"))


def mm_tn(a, b, out_dtype, name, b_cols=0, exch=None, tm=1024, tn=1024, tk=2048):
    JA, T, M = a.shape
    if b_cols:
        JB, N = b_cols, b.shape[1] // b_cols
    else:
        JB, _, N = b.shape
    J = max(JA, JB)
    tm = _lane_tile(M, tm)
    tn = _lane_tile(N, tn)
    tk = _row_tile(T, tk)

    def vmem_bytes(tk):
        return 2 * 2 * tk * (tm + tn) + 2 * 4 * tm * tn + 2 * jnp.dtype(out_dtype).itemsize * tm * tn

    while vmem_bytes(tk) > MM_VMEM_BUDGET_BYTES and tk % 16 == 0:
        tk //= 2
    nk, nn = T // tk, N // tn

    def body(a_ref, b_ref, o_ref, acc):
        k = pl.program_id(3)
        p = lax.dot_general(a_ref[...], b_ref[...], TN_DIMS, preferred_element_type=f32)

        @pl.when(k == 0)
        def _():
            acc[...] = p

        @pl.when(k > 0)
        def _():
            acc[...] += p

        @pl.when(k == nk - 1)
        def _():
            o_ref[...] = acc[...].astype(o_ref.dtype)

    a_spec = pl.BlockSpec((None, tk, tm), (lambda j, i, n, k: (j, k, i)) if JA > 1 else (lambda j, i, n, k: (0, k, i)))
    if b_cols:
        b_spec = pl.BlockSpec((tk, tn), lambda j, i, n, k: (k, j * nn + n))
    else:
        b_spec = pl.BlockSpec((None, tk, tn), (lambda j, i, n, k: (j, k, n)) if JB > 1 else (lambda j, i, n, k: (0, k, n)))
    return _run(
        body, (a, b), exch, name=name, grid=(J, M // tm, nn, nk), in_specs=[a_spec, b_spec],
        out_specs=pl.BlockSpec((None, tm, tn), lambda j, i, n, k: (j, i, n)),
        out_shape=jax.ShapeDtypeStruct((J, M, N), out_dtype), scratch_shapes=[pltpu.VMEM((tm, tn), f32)],
        sem=("parallel", "parallel", "parallel", "arbitrary"))


def ln_fwd(x, u, g, b, alpha, name, tm=256):
    T, D = x.shape
    tm = _row_tile(T, tm)

    def body(x_ref, u_ref, g_ref, b_ref, y_ref, yb_ref, xh_ref, rs_ref):
        s = alpha * x_ref[...] + u_ref[...]
        mu = jnp.mean(s, axis=-1, keepdims=True)
        c = s - mu
        var = jnp.mean(c * c, axis=-1, keepdims=True)
        r = lax.rsqrt(var + LN_EPS)
        xh = c * r
        y = xh * g_ref[...] + b_ref[...]
        y_ref[...] = y
        yb_ref[...] = y.astype(bf16)
        xh_ref[...] = xh
        rs_ref[...] = r

    row = pl.BlockSpec((tm, D), lambda i: (i, 0))
    vec = pl.BlockSpec((1, D), lambda i: (0, 0))
    return _pcall(
        body, name=name, grid=(T // tm,), in_specs=[row, row, vec, vec],
        out_specs=[row, row, row, pl.BlockSpec((tm, 1), lambda i: (i, 0))],
        out_shape=[jax.ShapeDtypeStruct((T, D), f32), jax.ShapeDtypeStruct((T, D), bf16),
                   jax.ShapeDtypeStruct((T, D), f32), jax.ShapeDtypeStruct((T, 1), f32)],
        compiler_params=_cp("parallel"))(x, u, g.reshape(1, D), b.reshape(1, D))


def ln_bwd(dy, res, alpha, xhat, rstd, g, name, tm=256):
    T, D = dy.shape
    tm = _row_tile(T, tm)
    n_dy = 1 if res is None else 2

    def body(*refs):
        dy_refs, (xh_ref, rs_ref, g_ref, ds_ref, dsb_ref, dg_ref, db_ref) = refs[:n_dy], refs[n_dy:]
        i = pl.program_id(0)
        dy_t, xh = dy_refs[0][...], xh_ref[...]
        if res is not None:
            dy_t = dy_t + alpha * dy_refs[1][...]
        dxh = dy_t * g_ref[...]
        m1 = jnp.mean(dxh, axis=-1, keepdims=True)
        m2 = jnp.mean(dxh * xh, axis=-1, keepdims=True)
        ds = rs_ref[...] * (dxh - m1 - xh * m2)
        ds_ref[...] = ds
        dsb_ref[...] = ds.astype(bf16)
        pg = jnp.sum(dy_t * xh, axis=0, keepdims=True)
        pb = jnp.sum(dy_t, axis=0, keepdims=True)

        @pl.when(i == 0)
        def _():
            dg_ref[...] = pg
            db_ref[...] = pb

        @pl.when(i > 0)
        def _():
            dg_ref[...] += pg
            db_ref[...] += pb

    row = pl.BlockSpec((tm, D), lambda i: (i, 0))
    vec = pl.BlockSpec((1, D), lambda i: (0, 0))
    dys = (dy,) if res is None else (dy, res)
    return _pcall(
        body, name=name, grid=(T // tm,), in_specs=[row] * n_dy + [row, pl.BlockSpec((tm, 1), lambda i: (i, 0)), vec],
        out_specs=[row, row, vec, vec],
        out_shape=[jax.ShapeDtypeStruct((T, D), f32), jax.ShapeDtypeStruct((T, D), bf16),
                   jax.ShapeDtypeStruct((1, D), f32), jax.ShapeDtypeStruct((1, D), f32)],
        compiler_params=_cp("arbitrary"))(*dys, xhat, rstd, g.reshape(1, D))


def loss_head(y, target, name, tm=256):
    T, D = y.shape
    tm = _row_tile(T, tm)
    nt = T // tm

    def body(y_ref, t_ref, dy_ref, l_ref, acc):
        i = pl.program_id(0)
        e = y_ref[...] - t_ref[...]
        dy_ref[...] = e / D
        p = jnp.sum(e * e, axis=0, keepdims=True)

        @pl.when(i == 0)
        def _():
            acc[...] = p

        @pl.when(i > 0)
        def _():
            acc[...] += p

        @pl.when(i == nt - 1)
        def _():
            l_ref[...] = jnp.full(l_ref.shape, 0.5 * jnp.sum(acc[...]) / D, f32)

    row = pl.BlockSpec((tm, D), lambda i: (i, 0))
    return _pcall(
        body, name=name, grid=(nt,), in_specs=[row, row],
        out_specs=[row, pl.BlockSpec((1, 128), lambda i: (0, 0))],
        out_shape=[jax.ShapeDtypeStruct((T, D), f32), jax.ShapeDtypeStruct((1, 128), f32)],
        scratch_shapes=[pltpu.VMEM((1, D), f32)], compiler_params=_cp("arbitrary"))(y, target)


def pool_fwd(x, w, scale, name, tm=256):
    T, D = x.shape
    G, C, _ = w.shape
    tm = _row_tile(T, tm)
    assert all(wd & (wd - 1) == 0 and wd - 1 <= POOL_HALO for wd in POOL_WINDOWS) and tm >= POOL_HALO

    def body(x_ref, w_ref, s_ref, u_ref, p_ref, halo):
        i = pl.program_id(0)

        @pl.when(i == 0)
        def _():
            halo[...] = jnp.zeros_like(halo)

        cur = x_ref[...]
        cat = jnp.concatenate([halo[...], cur], axis=0)
        halo[...] = cur[tm - POOL_HALO:, :]
        t1 = i * tm + lax.broadcasted_iota(jnp.int32, (tm, 1), 0) + 1
        for gi, wd in enumerate(POOL_WINDOWS):
            lo, hi = gi * C, (gi + 1) * C
            win = cat[:, lo:hi]
            sh = 1
            while sh < wd:
                win = win + pltpu.roll(win, sh, 0)
                sh *= 2
            cnt = jnp.minimum(t1, wd).astype(f32)
            pooled = (win[POOL_HALO:, :] / cnt - cur[:, lo:hi]).astype(bf16)
            p_ref[:, lo:hi] = pooled
            u_ref[:, lo:hi] = jnp.dot(pooled, w_ref[gi], preferred_element_type=f32) * s_ref[:, lo:hi]

    row = pl.BlockSpec((tm, D), lambda i: (i, 0))
    return _pcall(
        body, name=name, grid=(T // tm,),
        in_specs=[row, pl.BlockSpec((G, C, C), lambda i: (0, 0, 0)), pl.BlockSpec((1, D), lambda i: (0, 0))],
        out_specs=[row, row], out_shape=[jax.ShapeDtypeStruct((T, D), f32), jax.ShapeDtypeStruct((T, D), bf16)],
        scratch_shapes=[pltpu.VMEM((POOL_HALO, D), f32)], compiler_params=_cp("arbitrary"))(x, w, scale.reshape(1, D))


def pool_bwd(du, pooled, w, scale, alpha, name, tm=256):
    T, D = du.shape
    G, C, _ = w.shape
    tm = _row_tile(T, tm)
    nt = T // tm
    n = tm + POOL_HALO

    def body(du_ref, p_ref, w_ref, s_ref, dx_ref, dw_ref, dsc_ref, halo):
        i = pl.program_id(0)

        @pl.when(i == 0)
        def _():
            halo[...] = jnp.zeros_like(halo)
            dw_ref[...] = jnp.zeros_like(dw_ref)
            dsc_ref[...] = jnp.zeros_like(dsc_ref)

        t1 = (nt - 1 - i) * tm + lax.broadcasted_iota(jnp.int32, (tm, 1), 0) + 1
        for gi, wd in enumerate(POOL_WINDOWS):
            lo, hi = gi * C, (gi + 1) * C
            du_g, pb = du_ref[:, lo:hi], p_ref[:, lo:hi]
            yg = jnp.dot(pb, w_ref[gi], preferred_element_type=f32)
            dsc_ref[:, lo:hi] += jnp.sum(du_g * yg, axis=0, keepdims=True)
            dyg = (du_g * s_ref[:, lo:hi]).astype(bf16)
            dw_ref[gi] += lax.dot_general(pb, dyg, TN_DIMS, preferred_element_type=f32)
            dp = lax.dot_general(dyg, w_ref[gi], NT_DIMS, preferred_element_type=f32)
            e = dp / jnp.minimum(t1, wd).astype(f32)
            win = jnp.concatenate([e, halo[:, lo:hi]], axis=0)
            halo[:, lo:hi] = e[:POOL_HALO, :]
            sh = 1
            while sh < wd:
                win = win + pltpu.roll(win, n - sh, 0)
                sh *= 2
            dx_ref[:, lo:hi] = alpha * du_g + win[:tm, :] - dp

    row = pl.BlockSpec((tm, D), lambda i: (nt - 1 - i, 0))
    return _pcall(
        body, name=name, grid=(nt,),
        in_specs=[row, row, pl.BlockSpec((G, C, C), lambda i: (0, 0, 0)), pl.BlockSpec((1, D), lambda i: (0, 0))],
        out_specs=[row, pl.BlockSpec((G, C, C), lambda i: (0, 0, 0)), pl.BlockSpec((1, D), lambda i: (0, 0))],
        out_shape=[jax.ShapeDtypeStruct((T, D), f32), jax.ShapeDtypeStruct((G, C, C), f32), jax.ShapeDtypeStruct((1, D), f32)],
        scratch_shapes=[pltpu.VMEM((POOL_HALO, D), f32)], compiler_params=_cp("arbitrary"))(du, pooled, w, scale.reshape(1, D))


def _sigmoid(x):
    return 0.5 * jnp.tanh(0.5 * x) + 0.5


def _causal_conv(cur, prev, w_ref, b_ref):
    cat = jnp.concatenate([prev, cur], axis=0)
    h1 = pltpu.roll(cat, 1, 0)[CONV_HALO:, :]
    h2 = pltpu.roll(cat, 2, 0)[CONV_HALO:, :]
    out = b_ref[...] + w_ref[0:1, :] * h2
    out = out + w_ref[1:2, :] * h1
    out = out + w_ref[2:3, :] * cur
    return out, h1, h2


def gate_fwd(h, cw, cb, name, tm=256):
    J2, T, FB = h.shape
    J = J2 // 2
    tm = _row_tile(T, tm)

    def body(hg_ref, hv_ref, wg_ref, wv_ref, bg_ref, bv_ref, a_ref, halo_g, halo_v):
        i = pl.program_id(1)

        @pl.when(i == 0)
        def _():
            halo_g[...] = jnp.zeros_like(halo_g)
            halo_v[...] = jnp.zeros_like(halo_v)

        hg, hv = hg_ref[...], hv_ref[...]
        gate, _, _ = _causal_conv(hg, halo_g[...], wg_ref, bg_ref)
        val, _, _ = _causal_conv(hv, halo_v[...], wv_ref, bv_ref)
        halo_g[...] = hg[tm - CONV_HALO:, :]
        halo_v[...] = hv[tm - CONV_HALO:, :]
        a_ref[...] = (gate * _sigmoid(gate) * val).astype(bf16)

    def blk(rows, off):
        return pl.BlockSpec((None, rows, FB), lambda j, i: (j + off, i if rows == tm else 0, 0))

    return _pcall(
        body, name=name, grid=(J, T // tm),
        in_specs=[blk(tm, 0), blk(tm, J), blk(3, 0), blk(3, J), blk(1, 0), blk(1, J)],
        out_specs=pl.BlockSpec((None, tm, FB), lambda j, i: (j, i, 0)), out_shape=jax.ShapeDtypeStruct((J, T, FB), bf16),
        scratch_shapes=[pltpu.VMEM((CONV_HALO, FB), f32), pltpu.VMEM((CONV_HALO, FB), f32)],
        compiler_params=_cp("parallel", "arbitrary"))(h, h, cw, cw, cb, cb)


def gate_bwd(h, da, cw, cb, name, tm=256):
    J2, T, FB = h.shape
    J = J2 // 2
    tm = _row_tile(T, tm)
    nt = T // tm
    n = tm + CONV_HALO

    def body(hg_ref, hv_ref, pg_ref, pv_ref, da_ref, wg_ref, wv_ref, bg_ref, bv_ref,
             dhg_ref, dhv_ref, dwg_ref, dwv_ref, dbg_ref, dbv_ref, halo_g, halo_v):
        i = pl.program_id(1)

        @pl.when(i == 0)
        def _():
            for r in (halo_g, halo_v, dwg_ref, dwv_ref, dbg_ref, dbv_ref):
                r[...] = jnp.zeros_like(r)

        has_prev = (i < nt - 1).astype(f32)
        hg, hv = hg_ref[...], hv_ref[...]
        gate, g1, g2 = _causal_conv(hg, pg_ref[...] * has_prev, wg_ref, bg_ref)
        val, v1, v2 = _causal_conv(hv, pv_ref[...] * has_prev, wv_ref, bv_ref)
        sg = _sigmoid(gate)
        da_t = da_ref[...]
        dval = da_t * (gate * sg)
        dgate = da_t * val * (sg * (1.0 + gate * (1.0 - sg)))

        def back(d, cur, s1, s2, w_ref, dw_ref, db_ref, halo, dh_ref):
            db_ref[...] += jnp.sum(d, axis=0, keepdims=True)
            dw_ref[0:1, :] += jnp.sum(d * s2, axis=0, keepdims=True)
            dw_ref[1:2, :] += jnp.sum(d * s1, axis=0, keepdims=True)
            dw_ref[2:3, :] += jnp.sum(d * cur, axis=0, keepdims=True)
            cat = jnp.concatenate([d, halo[...]], axis=0)
            halo[...] = d[:CONV_HALO, :]
            d1 = pltpu.roll(cat, n - 1, 0)[:tm, :]
            d2 = pltpu.roll(cat, n - 2, 0)[:tm, :]
            dh = w_ref[2:3, :] * d + w_ref[1:2, :] * d1 + w_ref[0:1, :] * d2
            dh_ref[...] = dh.astype(bf16)

        back(dgate, hg, g1, g2, wg_ref, dwg_ref, dbg_ref, halo_g, dhg_ref)
        back(dval, hv, v1, v2, wv_ref, dwv_ref, dbv_ref, halo_v, dhv_ref)

    def rows(off):
        return pl.BlockSpec((None, tm, FB), lambda j, i: (j + off, nt - 1 - i, 0))

    def prev(off):
        return pl.BlockSpec((None, CONV_HALO, FB), lambda j, i: (j + off, jnp.maximum((nt - 1 - i) * (tm // CONV_HALO) - 1, 0), 0))

    def small(r, off):
        return pl.BlockSpec((None, r, FB), lambda j, i: (j + off, 0, 0))

    dhg, dhv, dwg, dwv, dbg, dbv = _pcall(
        body, name=name, grid=(J, nt),
        in_specs=[rows(0), rows(J), prev(0), prev(J), rows(0), small(3, 0), small(3, J), small(1, 0), small(1, J)],
        out_specs=[rows(0), rows(0), small(3, 0), small(3, 0), small(1, 0), small(1, 0)],
        out_shape=[jax.ShapeDtypeStruct((J, T, FB), bf16), jax.ShapeDtypeStruct((J, T, FB), bf16),
                   jax.ShapeDtypeStruct((J, 3, FB), f32), jax.ShapeDtypeStruct((J, 3, FB), f32),
                   jax.ShapeDtypeStruct((J, 1, FB), f32), jax.ShapeDtypeStruct((J, 1, FB), f32)],
        scratch_shapes=[pltpu.VMEM((CONV_HALO, FB), f32), pltpu.VMEM((CONV_HALO, FB), f32)],
        compiler_params=_cp("parallel", "arbitrary"))(h, h, h, h, da, cw, cw, cb, cb)
    return (jnp.concatenate([dhg, dhv], axis=0), jnp.concatenate([dwg, dwv], axis=0), jnp.concatenate([dbg, dbv], axis=0))


def _sum_rhs(tk, strict):
    r = lax.broadcasted_iota(jnp.int32, (2 * tk, 2 * tk), 0) % tk
    c = lax.broadcasted_iota(jnp.int32, (2 * tk, 2 * tk), 1)
    return jnp.logical_or(c >= tk, (r > c) if strict else (r >= c)).astype(bf16)


def _split_sums(x, rhs):
    hi = x.astype(bf16)
    lo = (x - hi.astype(f32)).astype(bf16)
    return jnp.dot(jnp.concatenate([hi, lo], axis=1), rhs, preferred_element_type=f32)


def _key_offset(tq, tk):
    return lax.broadcasted_iota(jnp.int32, (tq, tk), 1) - lax.broadcasted_iota(jnp.int32, (tq, tk), 0)


def _attn_geometry(T, tq, tk, nsub):
    tq = _row_tile(T, tq)
    nsub = max(1, min(nsub, T // tq))
    while T % (nsub * tq):
        nsub -= 1
    return tq, min(tk, tq), nsub


def _chain_block(it, nkb, tk):
    return it < nkb, pl.multiple_of(jnp.maximum(nkb - 1 - it, 0) * tk, tk)


def _sweep_live(it, rs, nkb):
    m = None
    for r, n in zip(rs, nkb):
        ra = jnp.where(it < n, r, 2.0 * EXP_UNDERFLOW)
        m = ra if m is None else jnp.maximum(m, ra)
    return jnp.max(m) > EXP_UNDERFLOW


def attn_fwd(qkv, name, exch=None, tq=256, tk=128, nsub=4):
    T, D3 = qkv.shape
    D = D3 // 3
    H = D // HEAD_DIM
    tq, tk, nsub = _attn_geometry(T, tq, tk, nsub)
    bq = nsub * tq
    scale = HEAD_DIM ** -0.5

    never = -(tq + tk)
    R = range(nsub)

    def body(q_ref, k_ref, v_ref, o_ref):
        i = pl.program_id(1)
        rhs = _sum_rhs(tk, False)
        cmr = _key_offset(tq, tk)
        nkb = [(i * nsub + a + 1) * (tq // tk) for a in R]
        qs = [q_ref[a * tq:(a + 1) * tq, :] for a in R]

        def cond(c):
            return _sweep_live(c[0], c[1], nkb)

        def step(c):
            it, rs, accs = c
            vk = [_chain_block(it, nkb[a], tk) for a in R]
            k0 = [vk[a][1] for a in R]
            off = (it + 1) * tk - tq
            zs = [lax.dot_general(qs[a], k_ref[pl.ds(k0[a], tk), :], NT_DIMS, preferred_element_type=f32) * scale for a in R]
            masks = [cmr < jnp.where(vk[a][0], off, never) for a in R]
            lsms = [jnp.where(masks[a], jnp.minimum(-zs[a], 0.0) - jnp.log(1.0 + jnp.exp(-jnp.abs(zs[a]))), 0.0) for a in R]
            lsum = [_split_sums(lsms[a], rhs) for a in R]
            ws = [jnp.where(masks[a], jnp.exp(zs[a] + lsum[a][:, :tk] + rs[a]), 0.0).astype(bf16) for a in R]
            accs2 = [accs[a] + jnp.dot(ws[a], v_ref[pl.ds(k0[a], tk), :], preferred_element_type=f32) for a in R]
            return it + 1, [rs[a] + lsum[a][:, tk:] for a in R], accs2

        init = (jnp.int32(0), [jnp.zeros((tq, tk), f32)] * nsub, [jnp.zeros((tq, HEAD_DIM), f32)] * nsub)
        _, _, accs = lax.while_loop(cond, step, init)
        for a in R:
            o_ref[a * tq:(a + 1) * tq, :] = accs[a].astype(bf16)

    return _run(
        body, (qkv, qkv, qkv), exch, name=name, grid=(H, T // bq),
        in_specs=[pl.BlockSpec((bq, HEAD_DIM), lambda h, i: (i, h)),
                  pl.BlockSpec((T, HEAD_DIM), lambda h, i: (0, H + h)),
                  pl.BlockSpec((T, HEAD_DIM), lambda h, i: (0, 2 * H + h))],
        out_specs=pl.BlockSpec((bq, HEAD_DIM), lambda h, i: (i, h)), out_shape=jax.ShapeDtypeStruct((T, D), bf16),
        sem=("parallel", "arbitrary"))


def attn_bwd(qkv, do, name, exch=None, tq=256, tk=128, nsub=4):
    T, D3 = qkv.shape
    D = D3 // 3
    H = D // HEAD_DIM
    tq, tk, nsub = _attn_geometry(T, tq, tk, nsub)
    bq = nsub * tq
    nq = T // bq
    scale = HEAD_DIM ** -0.5
    never = -(tq + tk)
    R = range(nsub)

    def body(q_ref, k_ref, v_ref, do_ref, dq_ref, dk_ref, dv_ref, dk_acc, dv_acc):
        i = pl.program_id(1)

        @pl.when(i == 0)
        def _():
            dk_acc[...] = jnp.zeros_like(dk_acc)
            dv_acc[...] = jnp.zeros_like(dv_acc)

        rhs_incl, rhs_excl = _sum_rhs(tk, False), _sum_rhs(tk, True)
        cmr = _key_offset(tq, tk)
        nkb = [(i * nsub + a + 1) * (tq // tk) for a in R]
        qs = [q_ref[a * tq:(a + 1) * tq, :] for a in R]
        dos = [do_ref[a * tq:(a + 1) * tq, :] for a in R]
        zeros = [jnp.zeros((tq, tk), f32)] * nsub

        def cond(c):
            return _sweep_live(c[0], c[1], nkb)

        def weights(it, rs):
            vk = [_chain_block(it, nkb[a], tk) for a in R]
            k0 = [vk[a][1] for a in R]
            off = (it + 1) * tk - tq
            zs = [lax.dot_general(qs[a], k_ref[pl.ds(k0[a], tk), :], NT_DIMS, preferred_element_type=f32) * scale for a in R]
            das = [lax.dot_general(dos[a], v_ref[pl.ds(k0[a], tk), :], NT_DIMS, preferred_element_type=f32) for a in R]
            masks = [cmr < jnp.where(vk[a][0], off, never) for a in R]
            es = [jnp.exp(-jnp.abs(zs[a])) for a in R]
            lsms = [jnp.where(masks[a], jnp.minimum(-zs[a], 0.0) - jnp.log(1.0 + es[a]), 0.0) for a in R]
            lsum = [_split_sums(lsms[a], rhs_incl) for a in R]
            ws = [jnp.where(masks[a], jnp.exp(zs[a] + lsum[a][:, :tk] + rs[a]), 0.0) for a in R]
            gs = [ws[a] * das[a] for a in R]
            gsum = [_split_sums(gs[a], rhs_excl) for a in R]
            return k0, masks, zs, es, ws, gs, lsum, gsum

        def total_step(c):
            it, rs, tots = c
            _, _, _, _, _, _, lsum, gsum = weights(it, rs)
            return it + 1, [rs[a] + lsum[a][:, tk:] for a in R], [tots[a] + gsum[a][:, tk:] for a in R]

        _, _, g_total = lax.while_loop(cond, total_step, (jnp.int32(0), zeros, zeros))

        def grad_step(c):
            it, rs, runs, dqs = c
            k0, masks, zs, es, ws, gs, lsum, gsum = weights(it, rs)
            sig = [jnp.where(zs[a] >= 0.0, 1.0, es[a]) / (1.0 + es[a]) for a in R]
            prefix = [g_total[a] - (runs[a] + gsum[a][:, :tk]) for a in R]
            dzs = [(jnp.where(masks[a], gs[a] - sig[a] * prefix[a], 0.0) * scale).astype(bf16) for a in R]
            wbs = [ws[a].astype(bf16) for a in R]
            dqs2 = [dqs[a] + jnp.dot(dzs[a], k_ref[pl.ds(k0[a], tk), :], preferred_element_type=f32) for a in R]
            dks = [lax.dot_general(dzs[a], qs[a], TN_DIMS, preferred_element_type=f32) for a in R]
            dvs = [lax.dot_general(wbs[a], dos[a], TN_DIMS, preferred_element_type=f32) for a in R]
            for a in R:
                dk_acc[pl.ds(k0[a], tk), :] += dks[a]
                dv_acc[pl.ds(k0[a], tk), :] += dvs[a]
            return it + 1, [rs[a] + lsum[a][:, tk:] for a in R], [runs[a] + gsum[a][:, tk:] for a in R], dqs2

        init = (jnp.int32(0), zeros, zeros, [jnp.zeros((tq, HEAD_DIM), f32)] * nsub)
        _, _, _, dqs = lax.while_loop(cond, grad_step, init)
        for a in R:
            dq_ref[a * tq:(a + 1) * tq, :] = dqs[a].astype(bf16)

        @pl.when(i == nq - 1)
        def _():
            dk_ref[...] = dk_acc[...].astype(bf16)
            dv_ref[...] = dv_acc[...].astype(bf16)

    qblk = pl.BlockSpec((bq, HEAD_DIM), lambda h, i: (i, h))
    head = pl.BlockSpec((T, HEAD_DIM), lambda h, i: (0, h))
    return _run(
        body, (qkv, qkv, qkv, do), exch, name=name, grid=(H, nq),
        in_specs=[qblk, pl.BlockSpec((T, HEAD_DIM), lambda h, i: (0, H + h)),
                  pl.BlockSpec((T, HEAD_DIM), lambda h, i: (0, 2 * H + h)), qblk],
        out_specs=[qblk, head, head], out_shape=[jax.ShapeDtypeStruct((T, D), bf16)] * 3,
        scratch_shapes=[pltpu.VMEM((T, HEAD_DIM), f32), pltpu.VMEM((T, HEAD_DIM), f32)],
        sem=("parallel", "arbitrary"))


def sum_partials(p, name, tr=256):
    K, R, C = p.shape
    tr = _row_tile(R, tr)

    def body(p_ref, o_ref):
        g = p_ref[0].astype(f32)
        for k in range(1, K):
            g = g + p_ref[k].astype(f32)
        o_ref[...] = g

    return _pcall(
        body, name=name, grid=(R // tr,), in_specs=[pl.BlockSpec((K, tr, C), lambda i: (0, i, 0))],
        out_specs=pl.BlockSpec((tr, C), lambda i: (i, 0)), out_shape=jax.ShapeDtypeStruct((R, C), f32),
        compiler_params=_cp("parallel"))(p)


def adamw(p, w, m, v, slot, outs, name):
    K, R, C = p.shape
    L = w.shape[0]
    tc = _lane_tile(C, 1024)
    row_bytes = 2 * tc * (K * p.dtype.itemsize + 7 * 4)
    tr = _row_tile(R, max(8, MM_VMEM_BUDGET_BYTES // row_bytes // 8 * 8))
    c1 = 1.0 - ADAM_B1 ** ADAM_STEP
    c2 = 1.0 - ADAM_B2 ** ADAM_STEP
    n_old = 0 if outs is None else 4

    def body(p_ref, w_ref, m_ref, v_ref, *refs):
        g_ref, d_ref, nm_ref, nv_ref = refs[n_old:]
        g = p_ref[0].astype(f32)
        for k in range(1, K):
            g = g + p_ref[k].astype(f32)
        nm = ADAM_B1 * m_ref[...] + (1.0 - ADAM_B1) * g
        nv = ADAM_B2 * v_ref[...] + (1.0 - ADAM_B2) * (g * g)
        g_ref[...] = g
        nm_ref[...] = nm
        nv_ref[...] = nv
        d_ref[...] = -ADAM_LR * ((nm / c1) / (jnp.sqrt(nv / c2) + ADAM_EPS) + ADAM_WD * w_ref[...])

    blk = pl.BlockSpec((None, tr, tc), lambda i, j: (slot, i, j))
    old = [] if outs is None else list(outs)
    return _pcall(
        body, name=name, grid=(R // tr, C // tc),
        in_specs=[pl.BlockSpec((K, tr, tc), lambda i, j: (0, i, j)), blk, blk, blk] + [pl.BlockSpec(memory_space=pl.ANY)] * n_old,
        out_specs=[blk] * 4, out_shape=[jax.ShapeDtypeStruct((L, R, C), f32)] * 4,
        input_output_aliases={4 + t: t for t in range(n_old)}, compiler_params=_cp("parallel", "parallel"))(p, w, m, v, *old)


def _pack(parts):
    flat, slices, off = [], [], 0
    for a in parts:
        nel = a.size
        pad = -nel % PACK_ALIGN
        flat.append(jnp.pad(a.reshape(-1).astype(f32), (0, pad)))
        slices.append((off, nel, a.shape))
        off += nel + pad
    return jnp.concatenate(flat).reshape(-1, 128), slices


def _unpack(packed, slices):
    flat = packed.reshape(-1)
    return [flat[off:off + nel].reshape(shape) for off, nel, shape in slices]


def kernel(x, pool_w, pool_scale, attn_w_qkv, attn_w_o, ffn_w_up, ffn_conv_w, ffn_conv_b, ffn_w_down, ln_mix_g, ln_mix_b, ln_ffn_g, ln_ffn_b, loss_target, m_pool_w, m_pool_scale, m_attn_w_qkv, m_attn_w_o, m_ffn_w_up, m_ffn_conv_w, m_ffn_conv_b, m_ffn_w_down, m_ln_mix_g, m_ln_mix_b, m_ln_ffn_g, m_ln_ffn_b, v_pool_w, v_pool_scale, v_attn_w_qkv, v_attn_w_o, v_ffn_w_up, v_ffn_conv_w, v_ffn_conv_b, v_ffn_w_down, v_ln_mix_g, v_ln_mix_b, v_ln_ffn_g, v_ln_ffn_b):
    _, T, D = x.shape
    depth = ln_mix_g.shape[0]
    alpha = (2.0 * depth) ** 0.25
    G, CS, C = pool_w.shape[1:]
    FB = ffn_w_up.shape[2]
    JH = N_DEV // 2
    me = 4 * lax.axis_index("x") + 2 * lax.axis_index("y") + lax.axis_index("c")

    up_t, m_up_t, v_up_t = (jnp.swapaxes(a, 1, 2) for a in (ffn_w_up, m_ffn_w_up, v_ffn_w_up))
    shards = {}
    for l in range(depth):
        j = l // 2
        shards["up", l], shards["down", l] = up_t[l].astype(bf16), ffn_w_down[l].astype(bf16)
        if l % 2 == 0:
            shards["pool", l] = pool_w[j].astype(bf16)
        else:
            shards["qkv", l], shards["wo", l] = attn_w_qkv[j].astype(bf16), attn_w_o[j].astype(bf16)
    full = {}

    def gather_in(call, keys):
        keys = [k for k in keys if k[1] < depth]
        if not keys:
            return call(None)
        res, got = call(([shards[k] for k in keys], False))
        full.update(zip(keys, got))
        return res

    first = [("pool", 0), ("up", 0)]
    got = all_gather([shards[k] for k in first] + [ffn_conv_w], "gather_first")
    full.update(zip(first, got))
    cw_all = got[-1]

    xf = x[0]
    xb = xf.astype(bf16)
    saved, weights = [], []
    for l in range(depth):
        j, sv, wl = l // 2, {}, {}
        if l % 2 == 0:
            wl["pool"] = full["pool", l].transpose(1, 0, 2, 3).reshape(G, C, C)
            u, sv["pooled"] = pool_fwd(xf, wl["pool"], pool_scale[j], f"pool_fwd_l{l}")
            up_rides = ([("down", l)] if l == 0 else [("qkv", l + 1)]) + [("wo", l + 1)]
            down_rides = [("qkv", l + 1)] if l == 0 else []
        else:
            wl["qkv"], wl["wo"] = full["qkv", l], full["wo", l].reshape(1, D, D)
            sv["xin_b"] = xb
            sv["qkv"] = mm_nn(xb, wl["qkv"], bf16, f"qkv_l{l}", out_cols=True)
            sv["o"] = gather_in(lambda e: attn_fwd(sv["qkv"], f"attn_fwd_l{l}", exch=e),
                                [("up", l), ("down", l), ("pool", l + 1)])
            u = mm_nn(sv["o"], wl["wo"], f32, f"attn_out_l{l}")[0]
            up_rides, down_rides = [("up", l + 1)], [("down", l + 1)]
        x1, sv["x1_b"], sv["xhat1"], sv["rstd1"] = ln_fwd(xf, u, ln_mix_g[l], ln_mix_b[l], alpha, f"ln_mix_l{l}")
        wl["up"] = full["up", l]
        sv["h"] = gather_in(lambda e: mm_nt(sv["x1_b"], wl["up"], f32, f"ffn_up_l{l}", exch=e), up_rides)
        wl["cw"], wl["cb"] = cw_all[:, l], ffn_conv_b[l].reshape(N_DEV, 1, FB)
        sv["a"] = gate_fwd(sv["h"], wl["cw"], wl["cb"], f"ffn_gate_l{l}")
        wl["down"] = full["down", l].reshape(JH, FB, D)
        f = gather_in(lambda e: mm_nn_sum(sv["a"], wl["down"], f"ffn_down_l{l}", exch=e), down_rides)
        xf, xb, sv["xhat2"], sv["rstd2"] = ln_fwd(x1, f, ln_ffn_g[l], ln_ffn_b[l], alpha, f"ln_ffn_l{l}")
        saved.append(sv)
        weights.append(wl)

    dx, loss_row = loss_head(xf, loss_target[0], "loss_head")

    landed = {}

    def scatter_in(call, parts):
        res, got = call(([p for _, p in parts], True))
        landed.update(zip([k for k, _ in parts], got))
        return res

    grads = [None] * depth
    res, pending = None, []
    for l in reversed(range(depth)):
        wl, sv, j, gl = weights[l], saved[l], l // 2, {}
        ds2, ds2_b, gl["ln_ffn_g"], gl["ln_ffn_b"] = ln_bwd(dx, res, alpha, sv["xhat2"], sv["rstd2"], ln_ffn_g[l], f"ln_ffn_bwd_l{l}")
        da = mm_nt(ds2_b, wl["down"], f32, f"ffn_down_dx_l{l}")
        g_down = mm_tn(sv["a"], ds2_b[None], bf16, f"ffn_down_dw_l{l}").reshape(N_DEV, FB // 2, D)
        dh, gl["cw"], gl["cb"] = gate_bwd(sv["h"], da, wl["cw"], wl["cb"], f"ffn_gate_bwd_l{l}")
        g_up = scatter_in(lambda e: mm_tn(dh, sv["x1_b"][None], bf16, f"ffn_up_dw_l{l}", exch=e),
                          [(("down", l), g_down)] + pending)
        dx1 = scatter_in(lambda e: mm_nn_sum(dh, wl["up"], f"ffn_up_dx_l{l}", exch=e), [(("up", l), g_up)])
        ds1, ds1_b, gl["ln_mix_g"], gl["ln_mix_b"] = ln_bwd(dx1, ds2, alpha, sv["xhat1"], sv["rstd1"], ln_mix_g[l], f"ln_mix_bwd_l{l}")
        if l % 2 == 0:
            dx, g_pool, gl["pool_scale"] = pool_bwd(ds1, sv["pooled"], wl["pool"], pool_scale[j], alpha, f"pool_bwd_l{l}")
            res, pending = None, [(("pool", l), g_pool.reshape(G, N_DEV, CS, C).transpose(1, 0, 2, 3).astype(bf16))]
        else:
            do = mm_nt(ds1_b, wl["wo"], bf16, f"attn_out_dx_l{l}")[0]
            g_wo = mm_tn(sv["o"][None], ds1_b[None], bf16, f"attn_out_dw_l{l}").reshape(N_DEV, D // N_DEV, D)
            dqkv = jnp.concatenate(scatter_in(lambda e: attn_bwd(sv["qkv"], do, f"attn_bwd_l{l}", exch=e),
                                              [(("wo", l), g_wo)]), axis=1)
            g_qkv = mm_tn(sv["xin_b"][None], dqkv, bf16, f"qkv_dw_l{l}", b_cols=N_DEV)
            dx = scatter_in(lambda e: mm_nt_sum(dqkv, wl["qkv"], f"qkv_dx_l{l}", a_cols=True, exch=e), [(("qkv", l), g_qkv)])
            res, pending = ds1, []
        grads[l] = gl
    assert res is None, "the first layer is a pooling layer: its backward adds the residual path itself"
    grad_x = dx[None]
    if pending:
        landed.update(zip([k for k, _ in pending], scatter_partials([p for _, p in pending], "scatter_last")))

    def stack_update(name, kind, w, m, v):
        shape = w.shape
        rows = w[0].size // shape[-1]
        flat = [a.reshape(shape[0], rows, shape[-1]) for a in (w, m, v)]
        outs = None
        for i, l in enumerate(sorted(l for k, l in landed if k == kind)):
            outs = adamw(landed[kind, l].reshape(N_DEV, rows, shape[-1]), *flat, i, outs, f"adamw_{name}_{i}")
        return [o.reshape(shape) for o in outs]

    big = {
        "pool_w": stack_update("pool_w", "pool", pool_w, m_pool_w, v_pool_w),
        "attn_w_qkv": stack_update("attn_w_qkv", "qkv", attn_w_qkv, m_attn_w_qkv, v_attn_w_qkv),
        "attn_w_o": stack_update("attn_w_o", "wo", attn_w_o, m_attn_w_o, v_attn_w_o),
        "ffn_w_up": [jnp.swapaxes(o, 1, 2) for o in stack_update("ffn_w_up", "up", up_t, m_up_t, v_up_t)],
        "ffn_w_down": stack_update("ffn_w_down", "down", ffn_w_down, m_ffn_w_down, v_ffn_w_down),
    }

    def per_layer(key):
        return jnp.stack([grads[l][key].reshape(-1) for l in range(depth)])

    small_parts = [per_layer("ln_mix_g"), per_layer("ln_mix_b"), per_layer("ln_ffn_g"), per_layer("ln_ffn_b"),
                   jnp.stack([grads[l]["pool_scale"].reshape(-1) for l in range(0, depth, 2)]),
                   per_layer("cb"), jnp.stack([grads[l]["cw"] for l in range(depth)]), loss_row[0, :1]]
    packed, slices = _pack(small_parts)
    total = sum_partials(all_gather([packed], "gather_small")[0], "sum_small")
    g_mix_g, g_mix_b, g_ffn_g, g_ffn_b, g_scale, g_cb, g_cw_all, loss = _unpack(total, slices)
    g_cw = lax.dynamic_index_in_dim(g_cw_all, me, axis=1, keepdims=False)

    small_w = [ln_mix_g, ln_mix_b, ln_ffn_g, ln_ffn_b, pool_scale, ffn_conv_b, ffn_conv_w]
    small_m = [m_ln_mix_g, m_ln_mix_b, m_ln_ffn_g, m_ln_ffn_b, m_pool_scale, m_ffn_conv_b, m_ffn_conv_w]
    small_v = [v_ln_mix_g, v_ln_mix_b, v_ln_ffn_g, v_ln_ffn_b, v_pool_scale, v_ffn_conv_b, v_ffn_conv_w]
    small_g = [g_mix_g, g_mix_b, g_ffn_g, g_ffn_b, g_scale, g_cb.reshape(ffn_conv_b.shape), g_cw]
    pw, wslices = _pack(small_w)
    res = adamw(_pack(small_g)[0][None], pw[None], _pack(small_m)[0][None], _pack(small_v)[0][None], 0, None, "adamw_small")
    small = dict(zip(["ln_mix_g", "ln_mix_b", "ln_ffn_g", "ln_ffn_b", "pool_scale", "ffn_conv_b", "ffn_conv_w"],
                     zip(*[_unpack(r[0], wslices) for r in res])))

    order = ["pool_w", "pool_scale", "attn_w_qkv", "attn_w_o", "ffn_w_up", "ffn_conv_w", "ffn_conv_b", "ffn_w_down",
             "ln_mix_g", "ln_mix_b", "ln_ffn_g", "ln_ffn_b"]
    table = {**big, **{k: list(val) for k, val in small.items()}}
    outs = [loss.reshape(()), grad_x]
    for t in range(4):
        outs += [table[name][t] for name in order]
    return tuple(outs)
```

```python
import jax
import jax.numpy as jnp
from jax import lax
from jax.experimental import pallas as pl
from jax.experimental.pallas import tpu as pltpu

f32, bf16 = jnp.float32, jnp.bfloat16

N_DEV = 8
HEAD_DIM = 128
POOL_WINDOWS = (2, 4, 8, 16)
POOL_HALO = 16
CONV_HALO = 8
LN_EPS = 1e-5
ADAM_LR, ADAM_B1, ADAM_B2, ADAM_EPS, ADAM_WD, ADAM_STEP = 0.001, 0.9, 0.999, 1e-08, 0.01, 10
V7X_VMEM_BYTES = 64 * 1024 * 1024
VMEM_LIMIT_BYTES = V7X_VMEM_BYTES - 8 * 1024 * 1024
MM_VMEM_BUDGET_BYTES = 40 * 1024 * 1024
EXP_UNDERFLOW = -104.0
PACK_ALIGN = 8 * 128

NT_DIMS = (((1,), (1,)), ((), ()))
TN_DIMS = (((0,), (0,)), ((), ()))


def _pcall(body, **kw):
    return pl.pallas_call(body, **kw)


def _cp(*sem):
    return pltpu.CompilerParams(dimension_semantics=sem, vmem_limit_bytes=VMEM_LIMIT_BYTES)


def _row_tile(rows, cap):
    if rows <= cap:
        return rows
    best = None
    for t in range(8, cap + 1, 8):
        if rows % t == 0:
            best = t
    assert best is not None, rows
    return best


def _lane_tile(cols, cap):
    best = cols
    for t in range(128, min(cap, cols) + 1, 128):
        if cols % t == 0:
            best = t
    return best if cols > cap else cols


N_PEER = N_DEV - 1


def _mesh_place():
    x, y, c = lax.axis_index("x"), lax.axis_index("y"), lax.axis_index("c")
    peers = []
    for k in range(1, N_DEV):
        px = 1 - x if k & 4 else x
        py = 1 - y if k & 2 else y
        pc = 1 - c if k & 1 else c
        peers.append(((px, py, pc), 4 * px + 2 * py + pc))
    return 4 * x + 2 * y + c, peers


def _exchange_copies(ins, outs, send, recv, loc, place, scatter):
    me, peers = place

    def src(a, idx):
        return ins[a].at[idx] if scatter else ins[a]

    local, starts, waits = [], [], []
    for a in range(len(ins)):
        local.append(pltpu.make_async_copy(src(a, me), outs[a].at[me], loc.at[a]))
        for k, (dev, idx) in enumerate(peers):
            sems = dict(send_sem=send.at[a * N_PEER + k], recv_sem=recv.at[a * N_PEER + k],
                        device_id=dev, device_id_type=pl.DeviceIdType.MESH)
            starts.append(pltpu.make_async_remote_copy(src_ref=src(a, idx), dst_ref=outs[a].at[me], **sems))
            waits.append(pltpu.make_async_remote_copy(src_ref=src(a, idx), dst_ref=outs[a].at[idx], **sems))
    return local, starts, waits


def _exchange_shapes(arrs, scatter):
    return [jax.ShapeDtypeStruct(a.shape if scatter else (N_DEV, *a.shape), a.dtype) for a in arrs]


def _exchange_sems(n):
    return [pltpu.SemaphoreType.DMA((n * N_PEER,)), pltpu.SemaphoreType.DMA((n * N_PEER,)), pltpu.SemaphoreType.DMA((n,))]


def _exchange(arrs, scatter, name):
    n = len(arrs)

    def body(*refs):
        local, starts, waits = _exchange_copies(refs[:n], refs[n:2 * n], *refs[2 * n:], _mesh_place(), scatter)
        for cp in local + starts:
            cp.start()
        for cp in waits + local:
            cp.wait()

    any_spec = pl.BlockSpec(memory_space=pl.ANY)
    return _pcall(body, name=name, in_specs=[any_spec] * n, out_specs=[any_spec] * n,
                  out_shape=_exchange_shapes(arrs, scatter), scratch_shapes=_exchange_sems(n))(*arrs)


def all_gather(shards, name):
    return _exchange(shards, False, name)


def scatter_partials(partials, name):
    return _exchange(partials, True, name)


def _run(body, args, exch, *, name, grid, in_specs, out_specs, out_shape, scratch_shapes=(), sem):
    scratch_shapes = list(scratch_shapes)
    if exch is None:
        return _pcall(body, name=name, grid=grid, in_specs=in_specs, out_specs=out_specs, out_shape=out_shape,
                      scratch_shapes=scratch_shapes, compiler_params=_cp(*sem))(*args)
    arrs, scatter = exch
    single = not isinstance(out_shape, (list, tuple))
    out_specs, out_shape = ([out_specs], [out_shape]) if single else (list(out_specs), list(out_shape))
    n, na, nb, ns = len(arrs), len(in_specs), len(out_specs), len(scratch_shapes)

    def hosted(*refs):
        ins, xin = refs[:na], refs[na:na + n]
        outs, xout = refs[na + n:na + n + nb], refs[na + n + nb:na + 2 * n + nb]
        scr, sems = refs[na + 2 * n + nb:na + 2 * n + nb + ns], refs[na + 2 * n + nb + ns:]
        first, last = None, None
        for d, size in enumerate(grid):
            pid = pl.program_id(d)
            first = (pid == 0) if first is None else jnp.logical_and(first, pid == 0)
            last = (pid == size - 1) if last is None else jnp.logical_and(last, pid == size - 1)
        local, starts, waits = _exchange_copies(xin, xout, *sems, _mesh_place(), scatter)

        @pl.when(first)
        def _():
            for cp in local + starts:
                cp.start()

        body(*ins, *outs, *scr)

        @pl.when(last)
        def _():
            for cp in waits + local:
                cp.wait()

    any_spec = pl.BlockSpec(memory_space=pl.ANY)
    res = _pcall(hosted, name=name, grid=grid, in_specs=list(in_specs) + [any_spec] * n,
                 out_specs=out_specs + [any_spec] * n, out_shape=out_shape + _exchange_shapes(arrs, scatter),
                 scratch_shapes=scratch_shapes + _exchange_sems(n),
                 compiler_params=_cp(*(["arbitrary"] * len(grid))))(*args, *arrs)
    return (res[0] if single else list(res[:nb])), list(res[nb:])


def mm_nn(a, b, out_dtype, name, out_cols=False, exch=None, tm=512):
    M, K = a.shape
    J, _, N = b.shape
    tm = _row_tile(M, tm)

    def body(a_ref, b_ref, o_ref):
        o_ref[...] = jnp.dot(a_ref[...], b_ref[...], preferred_element_type=f32).astype(o_ref.dtype)

    if out_cols:
        out_spec, out_shape = pl.BlockSpec((tm, N), lambda j, i: (i, j)), (M, J * N)
    else:
        out_spec, out_shape = pl.BlockSpec((None, tm, N), lambda j, i: (j, i, 0)), (J, M, N)
    return _run(
        body, (a, b), exch, name=name, grid=(J, M // tm),
        in_specs=[pl.BlockSpec((tm, K), lambda j, i: (i, 0)), pl.BlockSpec((None, K, N), lambda j, i: (j, 0, 0))],
        out_specs=out_spec, out_shape=jax.ShapeDtypeStruct(out_shape, out_dtype), sem=("parallel", "parallel"))


def mm_nn_sum(a, b, name, exch=None, tm=1024):
    J, M, K = a.shape
    N = b.shape[2]
    tm = _row_tile(M, tm)

    def body(a_ref, b_ref, o_ref):
        j = pl.program_id(1)
        p = jnp.dot(a_ref[...], b_ref[...], preferred_element_type=f32)

        @pl.when(j == 0)
        def _():
            o_ref[...] = p

        @pl.when(j > 0)
        def _():
            o_ref[...] += p

    return _run(
        body, (a, b), exch, name=name, grid=(M // tm, J),
        in_specs=[pl.BlockSpec((None, tm, K), lambda i, j: (j, i, 0)), pl.BlockSpec((None, K, N), lambda i, j: (j, 0, 0))],
        out_specs=pl.BlockSpec((tm, N), lambda i, j: (i, 0)), out_shape=jax.ShapeDtypeStruct((M, N), f32),
        sem=("parallel", "arbitrary"))


def mm_nt(a, b, out_dtype, name, exch=None, tm=512):
    M, K = a.shape
    J, N, _ = b.shape
    tm = _row_tile(M, tm)

    def body(a_ref, b_ref, o_ref):
        o_ref[...] = lax.dot_general(a_ref[...], b_ref[...], NT_DIMS, preferred_element_type=f32).astype(o_ref.dtype)

    return _run(
        body, (a, b), exch, name=name, grid=(J, M // tm),
        in_specs=[pl.BlockSpec((tm, K), lambda j, i: (i, 0)), pl.BlockSpec((None, N, K), lambda j, i: (j, 0, 0))],
        out_specs=pl.BlockSpec((None, tm, N), lambda j, i: (j, i, 0)), out_shape=jax.ShapeDtypeStruct((J, M, N), out_dtype),
        sem=("parallel", "parallel"))


def mm_nt_sum(a, b, name, a_cols=False, exch=None, tm=1024):
    J, N, K = b.shape
    M = a.shape[0] if a_cols else a.shape[1]
    tm = _row_tile(M, tm)

    def body(a_ref, b_ref, o_ref):
        j = pl.program_id(1)
        p = lax.dot_general(a_ref[...], b_ref[...], NT_DIMS, preferred_element_type=f32)

        @pl.when(j == 0)
        def _():
            o_ref[...] = p

        @pl.when(j > 0)
        def _():
            o_ref[...] += p

    a_spec = pl.BlockSpec((tm, K), lambda i, j: (i, j)) if a_cols else pl.BlockSpec((None, tm, K), lambda i, j: (j, i, 0))
    return _run(
        body, (a, b), exch, name=name, grid=(M // tm, J),
        in_specs=[a_spec, pl.BlockSpec((None, N, K), lambda i, j: (j, 0, 0))],
        out_specs=pl.BlockSpec((tm, N), lambda i, j: (i, 0)), out_shape=jax.ShapeDtypeStruct((M, N), f32),
        sem=("parallel", "arbitrary"))


def mm_tn(a, b, out_dtype, name, b_cols=0, exch=None, tm=1024, tn=1024, tk=2048):
    JA, T, M = a.shape
    if b_cols:
        JB, N = b_cols, b.shape[1] // b_cols
    else:
        JB, _, N = b.shape
    J = max(JA, JB)
    tm = _lane_tile(M, tm)
    tn = _lane_tile(N, tn)
    tk = _row_tile(T, tk)

    def vmem_bytes(tk):
        return 2 * 2 * tk * (tm + tn) + 2 * 4 * tm * tn + 2 * jnp.dtype(out_dtype).itemsize * tm * tn

    while vmem_bytes(tk) > MM_VMEM_BUDGET_BYTES and tk % 16 == 0:
        tk //= 2
    nk, nn = T // tk, N // tn

    def body(a_ref, b_ref, o_ref, acc):
        k = pl.program_id(3)
        p = lax.dot_general(a_ref[...], b_ref[...], TN_DIMS, preferred_element_type=f32)

        @pl.when(k == 0)
        def _():
            acc[...] = p

        @pl.when(k > 0)
        def _():
            acc[...] += p

        @pl.when(k == nk - 1)
        def _():
            o_ref[...] = acc[...].astype(o_ref.dtype)

    a_spec = pl.BlockSpec((None, tk, tm), (lambda j, i, n, k: (j, k, i)) if JA > 1 else (lambda j, i, n, k: (0, k, i)))
    if b_cols:
        b_spec = pl.BlockSpec((tk, tn), lambda j, i, n, k: (k, j * nn + n))
    else:
        b_spec = pl.BlockSpec((None, tk, tn), (lambda j, i, n, k: (j, k, n)) if JB > 1 else (lambda j, i, n, k: (0, k, n)))
    return _run(
        body, (a, b), exch, name=name, grid=(J, M // tm, nn, nk), in_specs=[a_spec, b_spec],
        out_specs=pl.BlockSpec((None, tm, tn), lambda j, i, n, k: (j, i, n)),
        out_shape=jax.ShapeDtypeStruct((J, M, N), out_dtype), scratch_shapes=[pltpu.VMEM((tm, tn), f32)],
        sem=("parallel", "parallel", "parallel", "arbitrary"))


def ln_fwd(x, u, g, b, alpha, name, tm=256):
    T, D = x.shape
    tm = _row_tile(T, tm)

    def body(x_ref, u_ref, g_ref, b_ref, y_ref, yb_ref, xh_ref, rs_ref):
        s = alpha * x_ref[...] + u_ref[...]
        mu = jnp.mean(s, axis=-1, keepdims=True)
        c = s - mu
        var = jnp.mean(c * c, axis=-1, keepdims=True)
        r = lax.rsqrt(var + LN_EPS)
        xh = c * r
        y = xh * g_ref[...] + b_ref[...]
        y_ref[...] = y
        yb_ref[...] = y.astype(bf16)
        xh_ref[...] = xh
        rs_ref[...] = r

    row = pl.BlockSpec((tm, D), lambda i: (i, 0))
    vec = pl.BlockSpec((1, D), lambda i: (0, 0))
    return _pcall(
        body, name=name, grid=(T // tm,), in_specs=[row, row, vec, vec],
        out_specs=[row, row, row, pl.BlockSpec((tm, 1), lambda i: (i, 0))],
        out_shape=[jax.ShapeDtypeStruct((T, D), f32), jax.ShapeDtypeStruct((T, D), bf16),
                   jax.ShapeDtypeStruct((T, D), f32), jax.ShapeDtypeStruct((T, 1), f32)],
        compiler_params=_cp("parallel"))(x, u, g.reshape(1, D), b.reshape(1, D))


def ln_bwd(dy, res, alpha, xhat, rstd, g, name, tm=256):
    T, D = dy.shape
    tm = _row_tile(T, tm)
    n_dy = 1 if res is None else 2

    def body(*refs):
        dy_refs, (xh_ref, rs_ref, g_ref, ds_ref, dsb_ref, dg_ref, db_ref) = refs[:n_dy], refs[n_dy:]
        i = pl.program_id(0)
        dy_t, xh = dy_refs[0][...], xh_ref[...]
        if res is not None:
            dy_t = dy_t + alpha * dy_refs[1][...]
        dxh = dy_t * g_ref[...]
        m1 = jnp.mean(dxh, axis=-1, keepdims=True)
        m2 = jnp.mean(dxh * xh, axis=-1, keepdims=True)
        ds = rs_ref[...] * (dxh - m1 - xh * m2)
        ds_ref[...] = ds
        dsb_ref[...] = ds.astype(bf16)
        pg = jnp.sum(dy_t * xh, axis=0, keepdims=True)
        pb = jnp.sum(dy_t, axis=0, keepdims=True)

        @pl.when(i == 0)
        def _():
            dg_ref[...] = pg
            db_ref[...] = pb

        @pl.when(i > 0)
        def _():
            dg_ref[...] += pg
            db_ref[...] += pb

    row = pl.BlockSpec((tm, D), lambda i: (i, 0))
    vec = pl.BlockSpec((1, D), lambda i: (0, 0))
    dys = (dy,) if res is None else (dy, res)
    return _pcall(
        body, name=name, grid=(T // tm,), in_specs=[row] * n_dy + [row, pl.BlockSpec((tm, 1), lambda i: (i, 0)), vec],
        out_specs=[row, row, vec, vec],
        out_shape=[jax.ShapeDtypeStruct((T, D), f32), jax.ShapeDtypeStruct((T, D), bf16),
                   jax.ShapeDtypeStruct((1, D), f32), jax.ShapeDtypeStruct((1, D), f32)],
        compiler_params=_cp("arbitrary"))(*dys, xhat, rstd, g.reshape(1, D))


def loss_head(y, target, name, tm=256):
    T, D = y.shape
    tm = _row_tile(T, tm)
    nt = T // tm

    def body(y_ref, t_ref, dy_ref, l_ref, acc):
        i = pl.program_id(0)
        e = y_ref[...] - t_ref[...]
        dy_ref[...] = e / D
        p = jnp.sum(e * e, axis=0, keepdims=True)

        @pl.when(i == 0)
        def _():
            acc[...] = p

        @pl.when(i > 0)
        def _():
            acc[...] += p

        @pl.when(i == nt - 1)
        def _():
            l_ref[...] = jnp.full(l_ref.shape, 0.5 * jnp.sum(acc[...]) / D, f32)

    row = pl.BlockSpec((tm, D), lambda i: (i, 0))
    return _pcall(
        body, name=name, grid=(nt,), in_specs=[row, row],
        out_specs=[row, pl.BlockSpec((1, 128), lambda i: (0, 0))],
        out_shape=[jax.ShapeDtypeStruct((T, D), f32), jax.ShapeDtypeStruct((1, 128), f32)],
        scratch_shapes=[pltpu.VMEM((1, D), f32)], compiler_params=_cp("arbitrary"))(y, target)


def pool_fwd(x, w, scale, name, tm=256):
    T, D = x.shape
    G, C, _ = w.shape
    tm = _row_tile(T, tm)
    assert all(wd & (wd - 1) == 0 and wd - 1 <= POOL_HALO for wd in POOL_WINDOWS) and tm >= POOL_HALO

    def body(x_ref, w_ref, s_ref, u_ref, p_ref, halo):
        i = pl.program_id(0)

        @pl.when(i == 0)
        def _():
            halo[...] = jnp.zeros_like(halo)

        cur = x_ref[...]
        cat = jnp.concatenate([halo[...], cur], axis=0)
        halo[...] = cur[tm - POOL_HALO:, :]
        t1 = i * tm + lax.broadcasted_iota(jnp.int32, (tm, 1), 0) + 1
        for gi, wd in enumerate(POOL_WINDOWS):
            lo, hi = gi * C, (gi + 1) * C
            win = cat[:, lo:hi]
            sh = 1
            while sh < wd:
                win = win + pltpu.roll(win, sh, 0)
                sh *= 2
            cnt = jnp.minimum(t1, wd).astype(f32)
            pooled = (win[POOL_HALO:, :] / cnt - cur[:, lo:hi]).astype(bf16)
            p_ref[:, lo:hi] = pooled
            u_ref[:, lo:hi] = jnp.dot(pooled, w_ref[gi], preferred_element_type=f32) * s_ref[:, lo:hi]

    row = pl.BlockSpec((tm, D), lambda i: (i, 0))
    return _pcall(
        body, name=name, grid=(T // tm,),
        in_specs=[row, pl.BlockSpec((G, C, C), lambda i: (0, 0, 0)), pl.BlockSpec((1, D), lambda i: (0, 0))],
        out_specs=[row, row], out_shape=[jax.ShapeDtypeStruct((T, D), f32), jax.ShapeDtypeStruct((T, D), bf16)],
        scratch_shapes=[pltpu.VMEM((POOL_HALO, D), f32)], compiler_params=_cp("arbitrary"))(x, w, scale.reshape(1, D))


def pool_bwd(du, pooled, w, scale, alpha, name, tm=256):
    T, D = du.shape
    G, C, _ = w.shape
    tm = _row_tile(T, tm)
    nt = T // tm
    n = tm + POOL_HALO

    def body(du_ref, p_ref, w_ref, s_ref, dx_ref, dw_ref, dsc_ref, halo):
        i = pl.program_id(0)

        @pl.when(i == 0)
        def _():
            halo[...] = jnp.zeros_like(halo)
            dw_ref[...] = jnp.zeros_like(dw_ref)
            dsc_ref[...] = jnp.zeros_like(dsc_ref)

        t1 = (nt - 1 - i) * tm + lax.broadcasted_iota(jnp.int32, (tm, 1), 0) + 1
        for gi, wd in enumerate(POOL_WINDOWS):
            lo, hi = gi * C, (gi + 1) * C
            du_g, pb = du_ref[:, lo:hi], p_ref[:, lo:hi]
            yg = jnp.dot(pb, w_ref[gi], preferred_element_type=f32)
            dsc_ref[:, lo:hi] += jnp.sum(du_g * yg, axis=0, keepdims=True)
            dyg = (du_g * s_ref[:, lo:hi]).astype(bf16)
            dw_ref[gi] += lax.dot_general(pb, dyg, TN_DIMS, preferred_element_type=f32)
            dp = lax.dot_general(dyg, w_ref[gi], NT_DIMS, preferred_element_type=f32)
            e = dp / jnp.minimum(t1, wd).astype(f32)
            win = jnp.concatenate([e, halo[:, lo:hi]], axis=0)
            halo[:, lo:hi] = e[:POOL_HALO, :]
            sh = 1
            while sh < wd:
                win = win + pltpu.roll(win, n - sh, 0)
                sh *= 2
            dx_ref[:, lo:hi] = alpha * du_g + win[:tm, :] - dp

    row = pl.BlockSpec((tm, D), lambda i: (nt - 1 - i, 0))
    return _pcall(
        body, name=name, grid=(nt,),
        in_specs=[row, row, pl.BlockSpec((G, C, C), lambda i: (0, 0, 0)), pl.BlockSpec((1, D), lambda i: (0, 0))],
        out_specs=[row, pl.BlockSpec((G, C, C), lambda i: (0, 0, 0)), pl.BlockSpec((1, D), lambda i: (0, 0))],
        out_shape=[jax.ShapeDtypeStruct((T, D), f32), jax.ShapeDtypeStruct((G, C, C), f32), jax.ShapeDtypeStruct((1, D), f32)],
        scratch_shapes=[pltpu.VMEM((POOL_HALO, D), f32)], compiler_params=_cp("arbitrary"))(du, pooled, w, scale.reshape(1, D))


def _sigmoid(x):
    return 0.5 * jnp.tanh(0.5 * x) + 0.5


def _halves(arr):
    J = arr.shape[0] // 2
    view = arr.reshape(2, J, *arr.shape[1:])

    def spec(rows, row_index):
        return pl.BlockSpec((2, None, rows, arr.shape[-1]), lambda j, i: (0, j, row_index(i), 0))

    return view, spec


def gate_fwd(h, cw, cb, name, tm=256):
    J2, T, FB = h.shape
    J = J2 // 2
    tm = _row_tile(T, tm)

    def body(h_ref, w_ref, b_ref, a_ref, c_ref, halo):
        i = pl.program_id(1)

        @pl.when(i == 0)
        def _():
            halo[...] = jnp.zeros_like(halo)

        conv = []
        for s in range(2):
            cur = h_ref[s]
            cat = jnp.concatenate([halo[s], cur], axis=0)
            halo[s] = cur[tm - CONV_HALO:, :]
            out = b_ref[s] + w_ref[s, 0:1, :] * pltpu.roll(cat, 2, 0)[CONV_HALO:, :]
            out = out + w_ref[s, 1:2, :] * pltpu.roll(cat, 1, 0)[CONV_HALO:, :]
            conv.append(out + w_ref[s, 2:3, :] * cur)
            c_ref[s] = conv[s]
        a_ref[...] = (conv[0] * _sigmoid(conv[0]) * conv[1]).astype(bf16)

    (h4, h_spec), (w4, w_spec), (b4, b_spec) = _halves(h), _halves(cw), _halves(cb)
    a, c = _pcall(
        body, name=name, grid=(J, T // tm),
        in_specs=[h_spec(tm, lambda i: i), w_spec(3, lambda i: 0), b_spec(1, lambda i: 0)],
        out_specs=[pl.BlockSpec((None, tm, FB), lambda j, i: (j, i, 0)), h_spec(tm, lambda i: i)],
        out_shape=[jax.ShapeDtypeStruct((J, T, FB), bf16), jax.ShapeDtypeStruct((2, J, T, FB), f32)],
        scratch_shapes=[pltpu.VMEM((2, CONV_HALO, FB), f32)], compiler_params=_cp("parallel", "arbitrary"))(h4, w4, b4)
    return a, c.reshape(J2, T, FB)


def gate_bwd(h, c, da, cw, name, tm=256):
    J2, T, FB = h.shape
    J = J2 // 2
    tm = _row_tile(T, tm)
    nt = T // tm
    n = tm + CONV_HALO

    def body(h_ref, c_ref, da_ref, w_ref, dh_ref, dw_ref, db_ref, halo):
        i = pl.program_id(1)

        @pl.when(i == 0)
        def _():
            for r in (halo, dw_ref, db_ref):
                r[...] = jnp.zeros_like(r)

        gate, val, da_t = c_ref[0], c_ref[1], da_ref[...]
        sg = _sigmoid(gate)
        ds = (da_t * val * (sg * (1.0 + gate * (1.0 - sg))), da_t * (gate * sg))
        for s in range(2):
            d, cur = ds[s], h_ref[s]
            cat = jnp.concatenate([d, halo[s]], axis=0)
            halo[s] = d[:CONV_HALO, :]
            d1 = pltpu.roll(cat, n - 1, 0)[:tm, :]
            d2 = pltpu.roll(cat, n - 2, 0)[:tm, :]
            db_ref[s] += jnp.sum(d, axis=0, keepdims=True)
            dw_ref[s, 0:1, :] += jnp.sum(d2 * cur, axis=0, keepdims=True)
            dw_ref[s, 1:2, :] += jnp.sum(d1 * cur, axis=0, keepdims=True)
            dw_ref[s, 2:3, :] += jnp.sum(d * cur, axis=0, keepdims=True)
            dh_ref[s] = (w_ref[s, 2:3, :] * d + w_ref[s, 1:2, :] * d1 + w_ref[s, 0:1, :] * d2).astype(bf16)

    (h4, h_spec), (c4, _), (w4, w_spec) = _halves(h), _halves(c), _halves(cw)
    rev = lambda i: nt - 1 - i
    b_spec = pl.BlockSpec((2, None, 1, FB), lambda j, i: (0, j, 0, 0))
    dh, dw, db = _pcall(
        body, name=name, grid=(J, nt),
        in_specs=[h_spec(tm, rev), h_spec(tm, rev), pl.BlockSpec((None, tm, FB), lambda j, i: (j, nt - 1 - i, 0)),
                  w_spec(3, lambda i: 0)],
        out_specs=[h_spec(tm, rev), w_spec(3, lambda i: 0), b_spec],
        out_shape=[jax.ShapeDtypeStruct((2, J, T, FB), bf16), jax.ShapeDtypeStruct((2, J, 3, FB), f32),
                   jax.ShapeDtypeStruct((2, J, 1, FB), f32)],
        scratch_shapes=[pltpu.VMEM((2, CONV_HALO, FB), f32)], compiler_params=_cp("parallel", "arbitrary"))(h4, c4, da, w4)
    return dh.reshape(J2, T, FB), dw.reshape(J2, 3, FB), db.reshape(J2, 1, FB)


def _sum_rhs(tk, strict):
    r = lax.broadcasted_iota(jnp.int32, (2 * tk, 2 * tk), 0) % tk
    c = lax.broadcasted_iota(jnp.int32, (2 * tk, 2 * tk), 1)
    return jnp.logical_or(c >= tk, (r > c) if strict else (r >= c)).astype(bf16)


def _split_sums(x, rhs):
    hi = x.astype(bf16)
    lo = (x - hi.astype(f32)).astype(bf16)
    return jnp.dot(jnp.concatenate([hi, lo], axis=1), rhs, preferred_element_type=f32)


def _key_offset(tq, tk):
    return lax.broadcasted_iota(jnp.int32, (tq, tk), 1) - lax.broadcasted_iota(jnp.int32, (tq, tk), 0)


def _attn_geometry(T, tq, tk, nsub):
    tq = _row_tile(T, tq)
    nsub = max(1, min(nsub, T // tq))
    while T % (nsub * tq):
        nsub -= 1
    return tq, min(tk, tq), nsub


def _chain_block(it, nkb, tk):
    return it < nkb, pl.multiple_of(jnp.maximum(nkb - 1 - it, 0) * tk, tk)


def _sweep_live(it, rs, nkb):
    m = None
    for r, n in zip(rs, nkb):
        ra = jnp.where(it < n, r, 2.0 * EXP_UNDERFLOW)
        m = ra if m is None else jnp.maximum(m, ra)
    return jnp.max(m) > EXP_UNDERFLOW


def attn_fwd(qkv, name, exch=None, tq=256, tk=128, nsub=4):
    T, D3 = qkv.shape
    D = D3 // 3
    H = D // HEAD_DIM
    tq, tk, nsub = _attn_geometry(T, tq, tk, nsub)
    bq = nsub * tq
    scale = HEAD_DIM ** -0.5

    never = -(tq + tk)
    R = range(nsub)

    def body(q_ref, k_ref, v_ref, o_ref, of_ref):
        i = pl.program_id(1)
        rhs = _sum_rhs(tk, False)
        cmr = _key_offset(tq, tk)
        nkb = [(i * nsub + a + 1) * (tq // tk) for a in R]
        qs = [q_ref[a * tq:(a + 1) * tq, :] for a in R]

        def cond(c):
            return _sweep_live(c[0], c[1], nkb)

        def step(c):
            it, rs, accs, fines = c
            vk = [_chain_block(it, nkb[a], tk) for a in R]
            k0 = [vk[a][1] for a in R]
            off = (it + 1) * tk - tq
            zs = [lax.dot_general(qs[a], k_ref[pl.ds(k0[a], tk), :], NT_DIMS, preferred_element_type=f32) * scale for a in R]
            masks = [cmr < jnp.where(vk[a][0], off, never) for a in R]
            lsms = [jnp.where(masks[a], jnp.minimum(-zs[a], 0.0) - jnp.log(1.0 + jnp.exp(-jnp.abs(zs[a]))), 0.0) for a in R]
            lsum = [_split_sums(lsms[a], rhs) for a in R]
            ws = [jnp.where(masks[a], jnp.exp(zs[a] + lsum[a][:, :tk] + rs[a]), 0.0) for a in R]
            his = [ws[a].astype(bf16) for a in R]
            los = [(ws[a] - his[a].astype(f32)).astype(bf16) for a in R]
            accs2 = [accs[a] + jnp.dot(his[a], v_ref[pl.ds(k0[a], tk), :], preferred_element_type=f32) for a in R]
            fines2 = [fines[a] + jnp.dot(los[a], v_ref[pl.ds(k0[a], tk), :], preferred_element_type=f32) for a in R]
            return it + 1, [rs[a] + lsum[a][:, tk:] for a in R], accs2, fines2

        zeros = [jnp.zeros((tq, HEAD_DIM), f32)] * nsub
        _, _, accs, fines = lax.while_loop(cond, step, (jnp.int32(0), [jnp.zeros((tq, tk), f32)] * nsub, zeros, zeros))
        for a in R:
            o_ref[a * tq:(a + 1) * tq, :] = accs[a].astype(bf16)
            of_ref[a * tq:(a + 1) * tq, :] = accs[a] + fines[a]

    qblk = pl.BlockSpec((bq, HEAD_DIM), lambda h, i: (i, h))
    return _run(
        body, (qkv, qkv, qkv), exch, name=name, grid=(H, T // bq),
        in_specs=[qblk, pl.BlockSpec((T, HEAD_DIM), lambda h, i: (0, H + h)),
                  pl.BlockSpec((T, HEAD_DIM), lambda h, i: (0, 2 * H + h))],
        out_specs=[qblk, qblk], out_shape=[jax.ShapeDtypeStruct((T, D), bf16), jax.ShapeDtypeStruct((T, D), f32)],
        sem=("parallel", "arbitrary"))


def attn_bwd(qkv, do, o_fine, name, exch=None, tq=256, tk=128, nsub=4):
    T, D3 = qkv.shape
    D = D3 // 3
    H = D // HEAD_DIM
    tq, tk, nsub = _attn_geometry(T, tq, tk, nsub)
    assert tk == HEAD_DIM
    bq = nsub * tq
    nq = T // bq
    scale = HEAD_DIM ** -0.5
    never = -(tq + tk)
    R = range(nsub)

    def body(q_ref, k_ref, v_ref, do_ref, of_ref, dq_ref, dk_ref, dv_ref, dk_acc, dv_acc):
        i = pl.program_id(1)

        @pl.when(i == 0)
        def _():
            dk_acc[...] = jnp.zeros_like(dk_acc)
            dv_acc[...] = jnp.zeros_like(dv_acc)

        rhs_incl, rhs_excl = _sum_rhs(tk, False), _sum_rhs(tk, True)
        cmr = _key_offset(tq, tk)
        nkb = [(i * nsub + a + 1) * (tq // tk) for a in R]
        qs = [q_ref[a * tq:(a + 1) * tq, :] for a in R]
        dos = [do_ref[a * tq:(a + 1) * tq, :] for a in R]
        zeros = [jnp.zeros((tq, tk), f32)] * nsub

        def cond(c):
            return _sweep_live(c[0], c[1], nkb)

        def weights(it, rs):
            vk = [_chain_block(it, nkb[a], tk) for a in R]
            k0 = [vk[a][1] for a in R]
            off = (it + 1) * tk - tq
            zs = [lax.dot_general(qs[a], k_ref[pl.ds(k0[a], tk), :], NT_DIMS, preferred_element_type=f32) * scale for a in R]
            das = [lax.dot_general(dos[a], v_ref[pl.ds(k0[a], tk), :], NT_DIMS, preferred_element_type=f32) for a in R]
            masks = [cmr < jnp.where(vk[a][0], off, never) for a in R]
            es = [jnp.exp(-jnp.abs(zs[a])) for a in R]
            lsms = [jnp.where(masks[a], jnp.minimum(-zs[a], 0.0) - jnp.log(1.0 + es[a]), 0.0) for a in R]
            lsum = [_split_sums(lsms[a], rhs_incl) for a in R]
            ws = [jnp.where(masks[a], jnp.exp(zs[a] + lsum[a][:, :tk] + rs[a]), 0.0) for a in R]
            gs = [ws[a] * das[a] for a in R]
            gsum = [_split_sums(gs[a], rhs_excl) for a in R]
            return k0, masks, zs, es, ws, gs, lsum, gsum

        g_total = [_split_sums(dos[a].astype(f32) * of_ref[a * tq:(a + 1) * tq, :], rhs_incl)[:, tk:] for a in R]

        def grad_step(c):
            it, rs, runs, dqs = c
            k0, masks, zs, es, ws, gs, lsum, gsum = weights(it, rs)
            sig = [jnp.where(zs[a] >= 0.0, 1.0, es[a]) / (1.0 + es[a]) for a in R]
            prefix = [g_total[a] - (runs[a] + gsum[a][:, :tk]) for a in R]
            dzs = [(jnp.where(masks[a], gs[a] - sig[a] * prefix[a], 0.0) * scale).astype(bf16) for a in R]
            wbs = [ws[a].astype(bf16) for a in R]
            dqs2 = [dqs[a] + jnp.dot(dzs[a], k_ref[pl.ds(k0[a], tk), :], preferred_element_type=f32) for a in R]
            dks = [lax.dot_general(dzs[a], qs[a], TN_DIMS, preferred_element_type=f32) for a in R]
            dvs = [lax.dot_general(wbs[a], dos[a], TN_DIMS, preferred_element_type=f32) for a in R]
            for a in R:
                dk_acc[pl.ds(k0[a], tk), :] += dks[a]
                dv_acc[pl.ds(k0[a], tk), :] += dvs[a]
            return it + 1, [rs[a] + lsum[a][:, tk:] for a in R], [runs[a] + gsum[a][:, tk:] for a in R], dqs2

        init = (jnp.int32(0), zeros, zeros, [jnp.zeros((tq, HEAD_DIM), f32)] * nsub)
        _, _, _, dqs = lax.while_loop(cond, grad_step, init)
        for a in R:
            dq_ref[a * tq:(a + 1) * tq, :] = dqs[a].astype(bf16)

        @pl.when(i == nq - 1)
        def _():
            dk_ref[...] = dk_acc[...].astype(bf16)
            dv_ref[...] = dv_acc[...].astype(bf16)

    qblk = pl.BlockSpec((bq, HEAD_DIM), lambda h, i: (i, h))
    head = pl.BlockSpec((T, HEAD_DIM), lambda h, i: (0, h))
    return _run(
        body, (qkv, qkv, qkv, do, o_fine), exch, name=name, grid=(H, nq),
        in_specs=[qblk, pl.BlockSpec((T, HEAD_DIM), lambda h, i: (0, H + h)),
                  pl.BlockSpec((T, HEAD_DIM), lambda h, i: (0, 2 * H + h)), qblk, qblk],
        out_specs=[qblk, head, head], out_shape=[jax.ShapeDtypeStruct((T, D), bf16)] * 3,
        scratch_shapes=[pltpu.VMEM((T, HEAD_DIM), f32), pltpu.VMEM((T, HEAD_DIM), f32)],
        sem=("parallel", "arbitrary"))


def sum_partials(p, name, tr=256):
    K, R, C = p.shape
    tr = _row_tile(R, tr)

    def body(p_ref, o_ref):
        g = p_ref[0].astype(f32)
        for k in range(1, K):
            g = g + p_ref[k].astype(f32)
        o_ref[...] = g

    return _pcall(
        body, name=name, grid=(R // tr,), in_specs=[pl.BlockSpec((K, tr, C), lambda i: (0, i, 0))],
        out_specs=pl.BlockSpec((tr, C), lambda i: (i, 0)), out_shape=jax.ShapeDtypeStruct((R, C), f32),
        compiler_params=_cp("parallel"))(p)


def adamw(p, w, m, v, slot, outs, name):
    K, R, C = p.shape
    L = w.shape[0]
    tc = _lane_tile(C, 1024)
    row_bytes = 2 * tc * (K * p.dtype.itemsize + 7 * 4)
    tr = _row_tile(R, max(8, MM_VMEM_BUDGET_BYTES // row_bytes // 8 * 8))
    c1 = 1.0 - ADAM_B1 ** ADAM_STEP
    c2 = 1.0 - ADAM_B2 ** ADAM_STEP
    n_old = 0 if outs is None else 4

    def body(p_ref, w_ref, m_ref, v_ref, *refs):
        g_ref, d_ref, nm_ref, nv_ref = refs[n_old:]
        g = p_ref[0].astype(f32)
        for k in range(1, K):
            g = g + p_ref[k].astype(f32)
        nm = ADAM_B1 * m_ref[...] + (1.0 - ADAM_B1) * g
        nv = ADAM_B2 * v_ref[...] + (1.0 - ADAM_B2) * (g * g)
        g_ref[...] = g
        nm_ref[...] = nm
        nv_ref[...] = nv
        d_ref[...] = -ADAM_LR * ((nm / c1) / (jnp.sqrt(nv / c2) + ADAM_EPS) + ADAM_WD * w_ref[...])

    blk = pl.BlockSpec((None, tr, tc), lambda i, j: (slot, i, j))
    old = [] if outs is None else list(outs)
    return _pcall(
        body, name=name, grid=(R // tr, C // tc),
        in_specs=[pl.BlockSpec((K, tr, tc), lambda i, j: (0, i, j)), blk, blk, blk] + [pl.BlockSpec(memory_space=pl.ANY)] * n_old,
        out_specs=[blk] * 4, out_shape=[jax.ShapeDtypeStruct((L, R, C), f32)] * 4,
        input_output_aliases={4 + t: t for t in range(n_old)}, compiler_params=_cp("parallel", "parallel"))(p, w, m, v, *old)


def _pack(parts):
    flat, slices, off = [], [], 0
    for a in parts:
        nel = a.size
        pad = -nel % PACK_ALIGN
        flat.append(jnp.pad(a.reshape(-1).astype(f32), (0, pad)))
        slices.append((off, nel, a.shape))
        off += nel + pad
    return jnp.concatenate(flat).reshape(-1, 128), slices


def _unpack(packed, slices):
    flat = packed.reshape(-1)
    return [flat[off:off + nel].reshape(shape) for off, nel, shape in slices]


def kernel(x, pool_w, pool_scale, attn_w_qkv, attn_w_o, ffn_w_up, ffn_conv_w, ffn_conv_b, ffn_w_down, ln_mix_g, ln_mix_b, ln_ffn_g, ln_ffn_b, loss_target, m_pool_w, m_pool_scale, m_attn_w_qkv, m_attn_w_o, m_ffn_w_up, m_ffn_conv_w, m_ffn_conv_b, m_ffn_w_down, m_ln_mix_g, m_ln_mix_b, m_ln_ffn_g, m_ln_ffn_b, v_pool_w, v_pool_scale, v_attn_w_qkv, v_attn_w_o, v_ffn_w_up, v_ffn_conv_w, v_ffn_conv_b, v_ffn_w_down, v_ln_mix_g, v_ln_mix_b, v_ln_ffn_g, v_ln_ffn_b):
    _, T, D = x.shape
    depth = ln_mix_g.shape[0]
    alpha = (2.0 * depth) ** 0.25
    G, CS, C = pool_w.shape[1:]
    FB = ffn_w_up.shape[2]
    JH = N_DEV // 2
    me = 4 * lax.axis_index("x") + 2 * lax.axis_index("y") + lax.axis_index("c")

    up_t, m_up_t, v_up_t = (jnp.swapaxes(a, 1, 2) for a in (ffn_w_up, m_ffn_w_up, v_ffn_w_up))
    shards = {}
    for l in range(depth):
        j = l // 2
        shards["up", l], shards["down", l] = up_t[l].astype(bf16), ffn_w_down[l].astype(bf16)
        if l % 2 == 0:
            shards["pool", l] = pool_w[j].astype(bf16)
        else:
            shards["qkv", l], shards["wo", l] = attn_w_qkv[j].astype(bf16), attn_w_o[j].astype(bf16)
    full = {}

    def gather_in(call, keys):
        keys = [k for k in keys if k[1] < depth]
        if not keys:
            return call(None)
        res, got = call(([shards[k] for k in keys], False))
        full.update(zip(keys, got))
        return res

    first = [("pool", 0), ("up", 0)]
    got = all_gather([shards[k] for k in first] + [ffn_conv_w], "gather_first")
    full.update(zip(first, got))
    cw_all = got[-1]

    xf = x[0]
    xb = xf.astype(bf16)
    saved, weights = [], []
    for l in range(depth):
        j, sv, wl = l // 2, {}, {}
        if l % 2 == 0:
            wl["pool"] = full["pool", l].transpose(1, 0, 2, 3).reshape(G, C, C)
            u, sv["pooled"] = pool_fwd(xf, wl["pool"], pool_scale[j], f"pool_fwd_l{l}")
            up_rides = ([("down", l)] if l == 0 else [("qkv", l + 1)]) + [("wo", l + 1)]
            down_rides = [("qkv", l + 1)] if l == 0 else []
        else:
            wl["qkv"], wl["wo"] = full["qkv", l], full["wo", l].reshape(1, D, D)
            sv["xin_b"] = xb
            sv["qkv"] = mm_nn(xb, wl["qkv"], bf16, f"qkv_l{l}", out_cols=True)
            sv["o"], sv["o_fine"] = gather_in(lambda e: attn_fwd(sv["qkv"], f"attn_fwd_l{l}", exch=e),
                                [("up", l), ("down", l), ("pool", l + 1)])
            u = mm_nn(sv["o"], wl["wo"], f32, f"attn_out_l{l}")[0]
            up_rides, down_rides = [("up", l + 1)], [("down", l + 1)]
        x1, sv["x1_b"], sv["xhat1"], sv["rstd1"] = ln_fwd(xf, u, ln_mix_g[l], ln_mix_b[l], alpha, f"ln_mix_l{l}")
        wl["up"] = full["up", l]
        sv["h"] = gather_in(lambda e: mm_nt(sv["x1_b"], wl["up"], f32, f"ffn_up_l{l}", exch=e), up_rides)
        wl["cw"], wl["cb"] = cw_all[:, l], ffn_conv_b[l].reshape(N_DEV, 1, FB)
        sv["a"], sv["c"] = gate_fwd(sv["h"], wl["cw"], wl["cb"], f"ffn_gate_l{l}")
        wl["down"] = full["down", l].reshape(JH, FB, D)
        f = gather_in(lambda e: mm_nn_sum(sv["a"], wl["down"], f"ffn_down_l{l}", exch=e), down_rides)
        xf, xb, sv["xhat2"], sv["rstd2"] = ln_fwd(x1, f, ln_ffn_g[l], ln_ffn_b[l], alpha, f"ln_ffn_l{l}")
        saved.append(sv)
        weights.append(wl)

    dx, loss_row = loss_head(xf, loss_target[0], "loss_head")

    landed = {}

    def scatter_in(call, parts):
        res, got = call(([p for _, p in parts], True))
        landed.update(zip([k for k, _ in parts], got))
        return res

    grads = [None] * depth
    res, pending = None, []
    for l in reversed(range(depth)):
        wl, sv, j, gl = weights[l], saved[l], l // 2, {}
        ds2, ds2_b, gl["ln_ffn_g"], gl["ln_ffn_b"] = ln_bwd(dx, res, alpha, sv["xhat2"], sv["rstd2"], ln_ffn_g[l], f"ln_ffn_bwd_l{l}")
        da = mm_nt(ds2_b, wl["down"], f32, f"ffn_down_dx_l{l}")
        g_down = mm_tn(sv["a"], ds2_b[None], bf16, f"ffn_down_dw_l{l}").reshape(N_DEV, FB // 2, D)
        dh, gl["cw"], gl["cb"] = gate_bwd(sv["h"], sv["c"], da, wl["cw"], f"ffn_gate_bwd_l{l}")
        g_up = scatter_in(lambda e: mm_tn(dh, sv["x1_b"][None], bf16, f"ffn_up_dw_l{l}", exch=e),
                          [(("down", l), g_down)] + pending)
        dx1 = scatter_in(lambda e: mm_nn_sum(dh, wl["up"], f"ffn_up_dx_l{l}", exch=e), [(("up", l), g_up)])
        ds1, ds1_b, gl["ln_mix_g"], gl["ln_mix_b"] = ln_bwd(dx1, ds2, alpha, sv["xhat1"], sv["rstd1"], ln_mix_g[l], f"ln_mix_bwd_l{l}")
        if l % 2 == 0:
            dx, g_pool, gl["pool_scale"] = pool_bwd(ds1, sv["pooled"], wl["pool"], pool_scale[j], alpha, f"pool_bwd_l{l}")
            res, pending = None, [(("pool", l), g_pool.reshape(G, N_DEV, CS, C).transpose(1, 0, 2, 3).astype(bf16))]
        else:
            do = mm_nt(ds1_b, wl["wo"], bf16, f"attn_out_dx_l{l}")[0]
            g_wo = mm_tn(sv["o"][None], ds1_b[None], bf16, f"attn_out_dw_l{l}").reshape(N_DEV, D // N_DEV, D)
            dqkv = jnp.concatenate(scatter_in(lambda e: attn_bwd(sv["qkv"], do, sv["o_fine"], f"attn_bwd_l{l}", exch=e),
                                              [(("wo", l), g_wo)]), axis=1)
            g_qkv = mm_tn(sv["xin_b"][None], dqkv, bf16, f"qkv_dw_l{l}", b_cols=N_DEV)
            dx = scatter_in(lambda e: mm_nt_sum(dqkv, wl["qkv"], f"qkv_dx_l{l}", a_cols=True, exch=e), [(("qkv", l), g_qkv)])
            res, pending = ds1, []
        grads[l] = gl
    assert res is None, "the first layer is a pooling layer: its backward adds the residual path itself"
    grad_x = dx[None]
    if pending:
        landed.update(zip([k for k, _ in pending], scatter_partials([p for _, p in pending], "scatter_last")))

    def stack_update(name, kind, w, m, v):
        shape = w.shape
        rows = w[0].size // shape[-1]
        flat = [a.reshape(shape[0], rows, shape[-1]) for a in (w, m, v)]
        outs = None
        for i, l in enumerate(sorted(l for k, l in landed if k == kind)):
            outs = adamw(landed[kind, l].reshape(N_DEV, rows, shape[-1]), *flat, i, outs, f"adamw_{name}_{i}")
        return [o.reshape(shape) for o in outs]

    big = {
        "pool_w": stack_update("pool_w", "pool", pool_w, m_pool_w, v_pool_w),
        "attn_w_qkv": stack_update("attn_w_qkv", "qkv", attn_w_qkv, m_attn_w_qkv, v_attn_w_qkv),
        "attn_w_o": stack_update("attn_w_o", "wo", attn_w_o, m_attn_w_o, v_attn_w_o),
        "ffn_w_up": [jnp.swapaxes(o, 1, 2) for o in stack_update("ffn_w_up", "up", up_t, m_up_t, v_up_t)],
        "ffn_w_down": stack_update("ffn_w_down", "down", ffn_w_down, m_ffn_w_down, v_ffn_w_down),
    }

    def per_layer(key):
        return jnp.stack([grads[l][key].reshape(-1) for l in range(depth)])

    small_parts = [per_layer("ln_mix_g"), per_layer("ln_mix_b"), per_layer("ln_ffn_g"), per_layer("ln_ffn_b"),
                   jnp.stack([grads[l]["pool_scale"].reshape(-1) for l in range(0, depth, 2)]),
                   per_layer("cb"), jnp.stack([grads[l]["cw"] for l in range(depth)]), loss_row[0, :1]]
    packed, slices = _pack(small_parts)
    total = sum_partials(all_gather([packed], "gather_small")[0], "sum_small")
    g_mix_g, g_mix_b, g_ffn_g, g_ffn_b, g_scale, g_cb, g_cw_all, loss = _unpack(total, slices)
    g_cw = lax.dynamic_index_in_dim(g_cw_all, me, axis=1, keepdims=False)

    small_w = [ln_mix_g, ln_mix_b, ln_ffn_g, ln_ffn_b, pool_scale, ffn_conv_b, ffn_conv_w]
    small_m = [m_ln_mix_g, m_ln_mix_b, m_ln_ffn_g, m_ln_ffn_b, m_pool_scale, m_ffn_conv_b, m_ffn_conv_w]
    small_v = [v_ln_mix_g, v_ln_mix_b, v_ln_ffn_g, v_ln_ffn_b, v_pool_scale, v_ffn_conv_b, v_ffn_conv_w]
    small_g = [g_mix_g, g_mix_b, g_ffn_g, g_ffn_b, g_scale, g_cb.reshape(ffn_conv_b.shape), g_cw]
    pw, wslices = _pack(small_w)
    res = adamw(_pack(small_g)[0][None], pw[None], _pack(small_m)[0][None], _pack(small_v)[0][None], 0, None, "adamw_small")
    small = dict(zip(["ln_mix_g", "ln_mix_b", "ln_ffn_g", "ln_ffn_b", "pool_scale", "ffn_conv_b", "ffn_conv_w"],
                     zip(*[_unpack(r[0], wslices) for r in res])))

    order = ["pool_w", "pool_scale", "attn_w_qkv", "attn_w_o", "ffn_w_up", "ffn_conv_w", "ffn_conv_b", "ffn_w_down",
             "ln_mix_g", "ln_mix_b", "ln_ffn_g", "ln_ffn_b"]
    table = {**big, **{k: list(val) for k, val in small.items()}}
    outs = [loss.reshape(()), grad_x]
    for t in range(4):
        outs += [table[name][t] for name in order]
    return tuple(outs)
```

```python
import jax
import jax.numpy as jnp
from jax import lax
from jax.experimental import pallas as pl
from jax.experimental.pallas import tpu as pltpu

f32, bf16 = jnp.float32, jnp.bfloat16

N_DEV = 8
HEAD_DIM = 128
POOL_WINDOWS = (2, 4, 8, 16)
POOL_HALO = 16
CONV_HALO = 8
LN_EPS = 1e-5
ADAM_LR, ADAM_B1, ADAM_B2, ADAM_EPS, ADAM_WD, ADAM_STEP = 0.001, 0.9, 0.999, 1e-08, 0.01, 10
V7X_VMEM_BYTES = 64 * 1024 * 1024
VMEM_LIMIT_BYTES = V7X_VMEM_BYTES - 8 * 1024 * 1024
MM_VMEM_BUDGET_BYTES = 40 * 1024 * 1024
EXP_UNDERFLOW = -104.0
PACK_ALIGN = 8 * 128

NT_DIMS = (((1,), (1,)), ((), ()))
TN_DIMS = (((0,), (0,)), ((), ()))


def _pcall(body, **kw):
    return pl.pallas_call(body, **kw)


def _cp(*sem):
    return pltpu.CompilerParams(dimension_semantics=sem, vmem_limit_bytes=VMEM_LIMIT_BYTES)


def _row_tile(rows, cap):
    if rows <= cap:
        return rows
    best = None
    for t in range(8, cap + 1, 8):
        if rows % t == 0:
            best = t
    assert best is not None, rows
    return best


def _lane_tile(cols, cap):
    best = cols
    for t in range(128, min(cap, cols) + 1, 128):
        if cols % t == 0:
            best = t
    return best if cols > cap else cols


N_PEER = N_DEV - 1


def _mesh_place():
    x, y, c = lax.axis_index("x"), lax.axis_index("y"), lax.axis_index("c")
    peers = []
    for k in range(1, N_DEV):
        px = 1 - x if k & 4 else x
        py = 1 - y if k & 2 else y
        pc = 1 - c if k & 1 else c
        peers.append(((px, py, pc), 4 * px + 2 * py + pc))
    return 4 * x + 2 * y + c, peers


def _exchange_copies(ins, outs, send, recv, loc, place, scatter):
    me, peers = place

    def src(a, idx):
        return ins[a].at[idx] if scatter else ins[a]

    local, starts, waits = [], [], []
    for a in range(len(ins)):
        local.append(pltpu.make_async_copy(src(a, me), outs[a].at[me], loc.at[a]))
        for k, (dev, idx) in enumerate(peers):
            sems = dict(send_sem=send.at[a * N_PEER + k], recv_sem=recv.at[a * N_PEER + k],
                        device_id=dev, device_id_type=pl.DeviceIdType.MESH)
            starts.append(pltpu.make_async_remote_copy(src_ref=src(a, idx), dst_ref=outs[a].at[me], **sems))
            waits.append(pltpu.make_async_remote_copy(src_ref=src(a, idx), dst_ref=outs[a].at[idx], **sems))
    return local, starts, waits


def _exchange_shapes(arrs, scatter):
    return [jax.ShapeDtypeStruct(a.shape if scatter else (N_DEV, *a.shape), a.dtype) for a in arrs]


def _exchange_sems(n):
    return [pltpu.SemaphoreType.DMA((n * N_PEER,)), pltpu.SemaphoreType.DMA((n * N_PEER,)), pltpu.SemaphoreType.DMA((n,))]


def _exchange(arrs, scatter, name):
    n = len(arrs)

    def body(*refs):
        local, starts, waits = _exchange_copies(refs[:n], refs[n:2 * n], *refs[2 * n:], _mesh_place(), scatter)
        for cp in local + starts:
            cp.start()
        for cp in waits + local:
            cp.wait()

    any_spec = pl.BlockSpec(memory_space=pl.ANY)
    return _pcall(body, name=name, in_specs=[any_spec] * n, out_specs=[any_spec] * n,
                  out_shape=_exchange_shapes(arrs, scatter), scratch_shapes=_exchange_sems(n))(*arrs)


def all_gather(shards, name):
    return _exchange(shards, False, name)


def scatter_partials(partials, name):
    return _exchange(partials, True, name)


def _run(body, args, exch, *, name, grid, in_specs, out_specs, out_shape, scratch_shapes=(), sem):
    scratch_shapes = list(scratch_shapes)
    if exch is None:
        return _pcall(body, name=name, grid=grid, in_specs=in_specs, out_specs=out_specs, out_shape=out_shape,
                      scratch_shapes=scratch_shapes, compiler_params=_cp(*sem))(*args)
    arrs, scatter = exch
    single = not isinstance(out_shape, (list, tuple))
    out_specs, out_shape = ([out_specs], [out_shape]) if single else (list(out_specs), list(out_shape))
    n, na, nb, ns = len(arrs), len(in_specs), len(out_specs), len(scratch_shapes)

    def hosted(*refs):
        ins, xin = refs[:na], refs[na:na + n]
        outs, xout = refs[na + n:na + n + nb], refs[na + n + nb:na + 2 * n + nb]
        scr, sems = refs[na + 2 * n + nb:na + 2 * n + nb + ns], refs[na + 2 * n + nb + ns:]
        first, last = None, None
        for d, size in enumerate(grid):
            pid = pl.program_id(d)
            first = (pid == 0) if first is None else jnp.logical_and(first, pid == 0)
            last = (pid == size - 1) if last is None else jnp.logical_and(last, pid == size - 1)
        local, starts, waits = _exchange_copies(xin, xout, *sems, _mesh_place(), scatter)

        @pl.when(first)
        def _():
            for cp in local + starts:
                cp.start()

        body(*ins, *outs, *scr)

        @pl.when(last)
        def _():
            for cp in waits + local:
                cp.wait()

    any_spec = pl.BlockSpec(memory_space=pl.ANY)
    res = _pcall(hosted, name=name, grid=grid, in_specs=list(in_specs) + [any_spec] * n,
                 out_specs=out_specs + [any_spec] * n, out_shape=out_shape + _exchange_shapes(arrs, scatter),
                 scratch_shapes=scratch_shapes + _exchange_sems(n),
                 compiler_params=_cp(*(["arbitrary"] * len(grid))))(*args, *arrs)
    return (res[0] if single else list(res[:nb])), list(res[nb:])


def mm_nn(a, b, out_dtype, name, out_cols=False, exch=None, tm=512):
    M, K = a.shape
    J, _, N = b.shape
    tm = _row_tile(M, tm)

    def body(a_ref, b_ref, o_ref):
        o_ref[...] = jnp.dot(a_ref[...], b_ref[...], preferred_element_type=f32).astype(o_ref.dtype)

    if out_cols:
        out_spec, out_shape = pl.BlockSpec((tm, N), lambda j, i: (i, j)), (M, J * N)
    else:
        out_spec, out_shape = pl.BlockSpec((None, tm, N), lambda j, i: (j, i, 0)), (J, M, N)
    return _run(
        body, (a, b), exch, name=name, grid=(J, M // tm),
        in_specs=[pl.BlockSpec((tm, K), lambda j, i: (i, 0)), pl.BlockSpec((None, K, N), lambda j, i: (j, 0, 0))],
        out_specs=out_spec, out_shape=jax.ShapeDtypeStruct(out_shape, out_dtype), sem=("parallel", "parallel"))


def mm_nn_sum(a, b, name, exch=None, tm=1024):
    J, M, K = a.shape
    N = b.shape[2]
    tm = _row_tile(M, tm)

    def body(a_ref, b_ref, o_ref):
        j = pl.program_id(1)
        p = jnp.dot(a_ref[...], b_ref[...], preferred_element_type=f32)

        @pl.when(j == 0)
        def _():
            o_ref[...] = p

        @pl.when(j > 0)
        def _():
            o_ref[...] += p

    return _run(
        body, (a, b), exch, name=name, grid=(M // tm, J),
        in_specs=[pl.BlockSpec((None, tm, K), lambda i, j: (j, i, 0)), pl.BlockSpec((None, K, N), lambda i, j: (j, 0, 0))],
        out_specs=pl.BlockSpec((tm, N), lambda i, j: (i, 0)), out_shape=jax.ShapeDtypeStruct((M, N), f32),
        sem=("parallel", "arbitrary"))


def mm_nt(a, b, out_dtype, name, exch=None, tm=512):
    M, K = a.shape
    J, N, _ = b.shape
    tm = _row_tile(M, tm)

    def body(a_ref, b_ref, o_ref):
        o_ref[...] = lax.dot_general(a_ref[...], b_ref[...], NT_DIMS, preferred_element_type=f32).astype(o_ref.dtype)

    return _run(
        body, (a, b), exch, name=name, grid=(J, M // tm),
        in_specs=[pl.BlockSpec((tm, K), lambda j, i: (i, 0)), pl.BlockSpec((None, N, K), lambda j, i: (j, 0, 0))],
        out_specs=pl.BlockSpec((None, tm, N), lambda j, i: (j, i, 0)), out_shape=jax.ShapeDtypeStruct((J, M, N), out_dtype),
        sem=("parallel", "parallel"))


def mm_nt_sum(a, b, name, a_cols=False, exch=None, tm=1024):
    J, N, K = b.shape
    M = a.shape[0] if a_cols else a.shape[1]
    tm = _row_tile(M, tm)

    def body(a_ref, b_ref, o_ref):
        j = pl.program_id(1)
        p = lax.dot_general(a_ref[...], b_ref[...], NT_DIMS, preferred_element_type=f32)

        @pl.when(j == 0)
        def _():
            o_ref[...] = p

        @pl.when(j > 0)
        def _():
            o_ref[...] += p

    a_spec = pl.BlockSpec((tm, K), lambda i, j: (i, j)) if a_cols else pl.BlockSpec((None, tm, K), lambda i, j: (j, i, 0))
    return _run(
        body, (a, b), exch, name=name, grid=(M // tm, J),
        in_specs=[a_spec, pl.BlockSpec((None, N, K), lambda i, j: (j, 0, 0))],
        out_specs=pl.BlockSpec((tm, N), lambda i, j: (i, 0)), out_shape=jax.ShapeDtypeStruct((M, N), f32),
        sem=("parallel", "arbitrary"))


def mm_tn(a, b, out_dtype, name, b_cols=0, exch=None, tm=1024, tn=1024, tk=2048):
    JA, T, M = a.shape
    if b_cols:
        JB, N = b_cols, b.shape[1] // b_cols
    else:
        JB, _, N = b.shape
    J = max(JA, JB)
    tm = _lane_tile(M, tm)
    tn = _lane_tile(N, tn)
    tk = _row_tile(T, tk)

    def vmem_bytes(tk):
        return 2 * 2 * tk * (tm + tn) + 2 * 4 * tm * tn + 2 * jnp.dtype(out_dtype).itemsize * tm * tn

    while vmem_bytes(tk) > MM_VMEM_BUDGET_BYTES and tk % 16 == 0:
        tk //= 2
    nk, nn = T // tk, N // tn

    def body(a_ref, b_ref, o_ref, acc):
        k = pl.program_id(3)
        p = lax.dot_general(a_ref[...], b_ref[...], TN_DIMS, preferred_element_type=f32)

        @pl.when(k == 0)
        def _():
            acc[...] = p

        @pl.when(k > 0)
        def _():
            acc[...] += p

        @pl.when(k == nk - 1)
        def _():
            o_ref[...] = acc[...].astype(o_ref.dtype)

    a_spec = pl.BlockSpec((None, tk, tm), (lambda j, i, n, k: (j, k, i)) if JA > 1 else (lambda j, i, n, k: (0, k, i)))
    if b_cols:
        b_spec = pl.BlockSpec((tk, tn), lambda j, i, n, k: (k, j * nn + n))
    else:
        b_spec = pl.BlockSpec((None, tk, tn), (lambda j, i, n, k: (j, k, n)) if JB > 1 else (lambda j, i, n, k: (0, k, n)))
    return _run(
        body, (a, b), exch, name=name, grid=(J, M // tm, nn, nk), in_specs=[a_spec, b_spec],
        out_specs=pl.BlockSpec((None, tm, tn), lambda j, i, n, k: (j, i, n)),
        out_shape=jax.ShapeDtypeStruct((J, M, N), out_dtype), scratch_shapes=[pltpu.VMEM((tm, tn), f32)],
        sem=("parallel", "parallel", "parallel", "arbitrary"))


def ln_fwd(x, u, g, b, alpha, name, tm=256):
    T, D = x.shape
    tm = _row_tile(T, tm)

    def body(x_ref, u_ref, g_ref, b_ref, y_ref, yb_ref, xh_ref, rs_ref):
        s = alpha * x_ref[...] + u_ref[...]
        mu = jnp.mean(s, axis=-1, keepdims=True)
        c = s - mu
        var = jnp.mean(c * c, axis=-1, keepdims=True)
        r = lax.rsqrt(var + LN_EPS)
        xh = c * r
        y = xh * g_ref[...] + b_ref[...]
        y_ref[...] = y
        yb_ref[...] = y.astype(bf16)
        xh_ref[...] = xh
        rs_ref[...] = r

    row = pl.BlockSpec((tm, D), lambda i: (i, 0))
    vec = pl.BlockSpec((1, D), lambda i: (0, 0))
    return _pcall(
        body, name=name, grid=(T // tm,), in_specs=[row, row, vec, vec],
        out_specs=[row, row, row, pl.BlockSpec((tm, 1), lambda i: (i, 0))],
        out_shape=[jax.ShapeDtypeStruct((T, D), f32), jax.ShapeDtypeStruct((T, D), bf16),
                   jax.ShapeDtypeStruct((T, D), f32), jax.ShapeDtypeStruct((T, 1), f32)],
        compiler_params=_cp("parallel"))(x, u, g.reshape(1, D), b.reshape(1, D))


def ln_bwd(dy, res, alpha, xhat, rstd, g, name, tm=256):
    T, D = dy.shape
    tm = _row_tile(T, tm)
    n_dy = 1 if res is None else 2

    def body(*refs):
        dy_refs, (xh_ref, rs_ref, g_ref, ds_ref, dsb_ref, dg_ref, db_ref) = refs[:n_dy], refs[n_dy:]
        i = pl.program_id(0)
        dy_t, xh = dy_refs[0][...], xh_ref[...]
        if res is not None:
            dy_t = dy_t + alpha * dy_refs[1][...]
        dxh = dy_t * g_ref[...]
        m1 = jnp.mean(dxh, axis=-1, keepdims=True)
        m2 = jnp.mean(dxh * xh, axis=-1, keepdims=True)
        ds = rs_ref[...] * (dxh - m1 - xh * m2)
        ds_ref[...] = ds
        dsb_ref[...] = ds.astype(bf16)
        pg = jnp.sum(dy_t * xh, axis=0, keepdims=True)
        pb = jnp.sum(dy_t, axis=0, keepdims=True)

        @pl.when(i == 0)
        def _():
            dg_ref[...] = pg
            db_ref[...] = pb

        @pl.when(i > 0)
        def _():
            dg_ref[...] += pg
            db_ref[...] += pb

    row = pl.BlockSpec((tm, D), lambda i: (i, 0))
    vec = pl.BlockSpec((1, D), lambda i: (0, 0))
    dys = (dy,) if res is None else (dy, res)
    return _pcall(
        body, name=name, grid=(T // tm,), in_specs=[row] * n_dy + [row, pl.BlockSpec((tm, 1), lambda i: (i, 0)), vec],
        out_specs=[row, row, vec, vec],
        out_shape=[jax.ShapeDtypeStruct((T, D), f32), jax.ShapeDtypeStruct((T, D), bf16),
                   jax.ShapeDtypeStruct((1, D), f32), jax.ShapeDtypeStruct((1, D), f32)],
        compiler_params=_cp("arbitrary"))(*dys, xhat, rstd, g.reshape(1, D))


def loss_head(y, target, name, tm=256):
    T, D = y.shape
    tm = _row_tile(T, tm)
    nt = T // tm

    def body(y_ref, t_ref, dy_ref, l_ref, acc):
        i = pl.program_id(0)
        e = y_ref[...] - t_ref[...]
        dy_ref[...] = e / D
        p = jnp.sum(e * e, axis=0, keepdims=True)

        @pl.when(i == 0)
        def _():
            acc[...] = p

        @pl.when(i > 0)
        def _():
            acc[...] += p

        @pl.when(i == nt - 1)
        def _():
            l_ref[...] = jnp.full(l_ref.shape, 0.5 * jnp.sum(acc[...]) / D, f32)

    row = pl.BlockSpec((tm, D), lambda i: (i, 0))
    return _pcall(
        body, name=name, grid=(nt,), in_specs=[row, row],
        out_specs=[row, pl.BlockSpec((1, 128), lambda i: (0, 0))],
        out_shape=[jax.ShapeDtypeStruct((T, D), f32), jax.ShapeDtypeStruct((1, 128), f32)],
        scratch_shapes=[pltpu.VMEM((1, D), f32)], compiler_params=_cp("arbitrary"))(y, target)


def pool_fwd(x, w, scale, name, tm=256):
    T, D = x.shape
    G, C, _ = w.shape
    tm = _row_tile(T, tm)
    assert all(wd & (wd - 1) == 0 and wd - 1 <= POOL_HALO for wd in POOL_WINDOWS) and tm >= POOL_HALO

    def body(x_ref, w_ref, s_ref, u_ref, p_ref, halo):
        i = pl.program_id(0)

        @pl.when(i == 0)
        def _():
            halo[...] = jnp.zeros_like(halo)

        cur = x_ref[...]
        cat = jnp.concatenate([halo[...], cur], axis=0)
        halo[...] = cur[tm - POOL_HALO:, :]
        t1 = i * tm + lax.broadcasted_iota(jnp.int32, (tm, 1), 0) + 1
        for gi, wd in enumerate(POOL_WINDOWS):
            lo, hi = gi * C, (gi + 1) * C
            win = cat[:, lo:hi]
            sh = 1
            while sh < wd:
                win = win + pltpu.roll(win, sh, 0)
                sh *= 2
            cnt = jnp.minimum(t1, wd).astype(f32)
            pooled = (win[POOL_HALO:, :] / cnt - cur[:, lo:hi]).astype(bf16)
            p_ref[:, lo:hi] = pooled
            u_ref[:, lo:hi] = jnp.dot(pooled, w_ref[gi], preferred_element_type=f32) * s_ref[:, lo:hi]

    row = pl.BlockSpec((tm, D), lambda i: (i, 0))
    return _pcall(
        body, name=name, grid=(T // tm,),
        in_specs=[row, pl.BlockSpec((G, C, C), lambda i: (0, 0, 0)), pl.BlockSpec((1, D), lambda i: (0, 0))],
        out_specs=[row, row], out_shape=[jax.ShapeDtypeStruct((T, D), f32), jax.ShapeDtypeStruct((T, D), bf16)],
        scratch_shapes=[pltpu.VMEM((POOL_HALO, D), f32)], compiler_params=_cp("arbitrary"))(x, w, scale.reshape(1, D))


def pool_bwd(du, pooled, w, scale, alpha, name, tm=256):
    T, D = du.shape
    G, C, _ = w.shape
    tm = _row_tile(T, tm)
    nt = T // tm
    n = tm + POOL_HALO

    def body(du_ref, p_ref, w_ref, s_ref, dx_ref, dw_ref, dsc_ref, halo):
        i = pl.program_id(0)

        @pl.when(i == 0)
        def _():
            halo[...] = jnp.zeros_like(halo)
            dw_ref[...] = jnp.zeros_like(dw_ref)
            dsc_ref[...] = jnp.zeros_like(dsc_ref)

        t1 = (nt - 1 - i) * tm + lax.broadcasted_iota(jnp.int32, (tm, 1), 0) + 1
        for gi, wd in enumerate(POOL_WINDOWS):
            lo, hi = gi * C, (gi + 1) * C
            du_g, pb = du_ref[:, lo:hi], p_ref[:, lo:hi]
            yg = jnp.dot(pb, w_ref[gi], preferred_element_type=f32)
            dsc_ref[:, lo:hi] += jnp.sum(du_g * yg, axis=0, keepdims=True)
            dyg = (du_g * s_ref[:, lo:hi]).astype(bf16)
            dw_ref[gi] += lax.dot_general(pb, dyg, TN_DIMS, preferred_element_type=f32)
            dp = lax.dot_general(dyg, w_ref[gi], NT_DIMS, preferred_element_type=f32)
            e = dp / jnp.minimum(t1, wd).astype(f32)
            win = jnp.concatenate([e, halo[:, lo:hi]], axis=0)
            halo[:, lo:hi] = e[:POOL_HALO, :]
            sh = 1
            while sh < wd:
                win = win + pltpu.roll(win, n - sh, 0)
                sh *= 2
            dx_ref[:, lo:hi] = alpha * du_g + win[:tm, :] - dp

    row = pl.BlockSpec((tm, D), lambda i: (nt - 1 - i, 0))
    return _pcall(
        body, name=name, grid=(nt,),
        in_specs=[row, row, pl.BlockSpec((G, C, C), lambda i: (0, 0, 0)), pl.BlockSpec((1, D), lambda i: (0, 0))],
        out_specs=[row, pl.BlockSpec((G, C, C), lambda i: (0, 0, 0)), pl.BlockSpec((1, D), lambda i: (0, 0))],
        out_shape=[jax.ShapeDtypeStruct((T, D), f32), jax.ShapeDtypeStruct((G, C, C), f32), jax.ShapeDtypeStruct((1, D), f32)],
        scratch_shapes=[pltpu.VMEM((POOL_HALO, D), f32)], compiler_params=_cp("arbitrary"))(du, pooled, w, scale.reshape(1, D))


def _sigmoid(x):
    return 0.5 * jnp.tanh(0.5 * x) + 0.5


def _halves(arr):
    J = arr.shape[0] // 2
    view = arr.reshape(2, J, *arr.shape[1:])

    def spec(rows, row_index):
        return pl.BlockSpec((2, None, rows, arr.shape[-1]), lambda j, i: (0, j, row_index(i), 0))

    return view, spec


def ffn_up_gate(x, w, cw, cb, name, exch=None, tm=256):
    T, D = x.shape
    J2, FB, _ = w.shape
    J = J2 // 2
    tm = _row_tile(T, tm)

    def body(x_ref, u_ref, w_ref, b_ref, h_ref, c_ref, a_ref, halo):
        i = pl.program_id(1)

        @pl.when(i == 0)
        def _():
            halo[...] = jnp.zeros_like(halo)

        x_t = x_ref[...]
        conv = []
        for s in range(2):
            cur = lax.dot_general(x_t, u_ref[s], NT_DIMS, preferred_element_type=f32)
            h_ref[s] = cur
            cat = jnp.concatenate([halo[s], cur], axis=0)
            halo[s] = cur[tm - CONV_HALO:, :]
            out = b_ref[s] + w_ref[s, 0:1, :] * pltpu.roll(cat, 2, 0)[CONV_HALO:, :]
            out = out + w_ref[s, 1:2, :] * pltpu.roll(cat, 1, 0)[CONV_HALO:, :]
            conv.append(out + w_ref[s, 2:3, :] * cur)
            c_ref[s] = conv[s]
        a_ref[...] = (conv[0] * _sigmoid(conv[0]) * conv[1]).astype(bf16)

    (u4, u_spec), (w4, w_spec), (b4, b_spec) = _halves(w), _halves(cw), _halves(cb)
    hc_spec = pl.BlockSpec((2, None, tm, FB), lambda j, i: (0, j, i, 0))
    hc_shape = jax.ShapeDtypeStruct((2, J, T, FB), f32)
    res = _run(
        body, (x, u4, w4, b4), exch, name=name, grid=(J, T // tm),
        in_specs=[pl.BlockSpec((tm, D), lambda j, i: (i, 0)), u_spec(FB, lambda i: 0), w_spec(3, lambda i: 0), b_spec(1, lambda i: 0)],
        out_specs=[hc_spec, hc_spec, pl.BlockSpec((None, tm, FB), lambda j, i: (j, i, 0))],
        out_shape=[hc_shape, hc_shape, jax.ShapeDtypeStruct((J, T, FB), bf16)],
        scratch_shapes=[pltpu.VMEM((2, CONV_HALO, FB), f32)], sem=("parallel", "arbitrary"))
    (h, c, a), got = res if exch is not None else (res, None)
    out = (h.reshape(J2, T, FB), c.reshape(J2, T, FB), a)
    return out if exch is None else (out, got)


def ffn_down_gate_bwd(ds, wd, h, c, cw, name, tm=256):
    J2, T, FB = h.shape
    J = J2 // 2
    D = ds.shape[1]
    tm = _row_tile(T, tm)
    nt = T // tm
    n = tm + CONV_HALO

    def body(ds_ref, wd_ref, h_ref, c_ref, w_ref, dh_ref, dw_ref, db_ref, halo):
        i = pl.program_id(1)

        @pl.when(i == 0)
        def _():
            for r in (halo, dw_ref, db_ref):
                r[...] = jnp.zeros_like(r)

        da_t = lax.dot_general(ds_ref[...], wd_ref[...], NT_DIMS, preferred_element_type=f32)
        gate, val = c_ref[0], c_ref[1]
        sg = _sigmoid(gate)
        ds = (da_t * val * (sg * (1.0 + gate * (1.0 - sg))), da_t * (gate * sg))
        for s in range(2):
            d, cur = ds[s], h_ref[s]
            cat = jnp.concatenate([d, halo[s]], axis=0)
            halo[s] = d[:CONV_HALO, :]
            d1 = pltpu.roll(cat, n - 1, 0)[:tm, :]
            d2 = pltpu.roll(cat, n - 2, 0)[:tm, :]
            db_ref[s] += jnp.sum(d, axis=0, keepdims=True)
            dw_ref[s, 0:1, :] += jnp.sum(d2 * cur, axis=0, keepdims=True)
            dw_ref[s, 1:2, :] += jnp.sum(d1 * cur, axis=0, keepdims=True)
            dw_ref[s, 2:3, :] += jnp.sum(d * cur, axis=0, keepdims=True)
            dh_ref[s] = (w_ref[s, 2:3, :] * d + w_ref[s, 1:2, :] * d1 + w_ref[s, 0:1, :] * d2).astype(bf16)

    (h4, h_spec), (c4, _), (w4, w_spec) = _halves(h), _halves(c), _halves(cw)
    rev = lambda i: nt - 1 - i
    b_spec = pl.BlockSpec((2, None, 1, FB), lambda j, i: (0, j, 0, 0))
    dh, dw, db = _pcall(
        body, name=name, grid=(J, nt),
        in_specs=[pl.BlockSpec((tm, D), lambda j, i: (nt - 1 - i, 0)), pl.BlockSpec((None, FB, D), lambda j, i: (j, 0, 0)),
                  h_spec(tm, rev), h_spec(tm, rev), w_spec(3, lambda i: 0)],
        out_specs=[h_spec(tm, rev), w_spec(3, lambda i: 0), b_spec],
        out_shape=[jax.ShapeDtypeStruct((2, J, T, FB), bf16), jax.ShapeDtypeStruct((2, J, 3, FB), f32),
                   jax.ShapeDtypeStruct((2, J, 1, FB), f32)],
        scratch_shapes=[pltpu.VMEM((2, CONV_HALO, FB), f32)], compiler_params=_cp("parallel", "arbitrary"))(ds, wd, h4, c4, w4)
    return dh.reshape(J2, T, FB), dw.reshape(J2, 3, FB), db.reshape(J2, 1, FB)


def _sum_rhs(tk, strict):
    r = lax.broadcasted_iota(jnp.int32, (2 * tk, 2 * tk), 0) % tk
    c = lax.broadcasted_iota(jnp.int32, (2 * tk, 2 * tk), 1)
    return jnp.logical_or(c >= tk, (r > c) if strict else (r >= c)).astype(bf16)


def _split_sums(x, rhs):
    hi = x.astype(bf16)
    lo = (x - hi.astype(f32)).astype(bf16)
    return jnp.dot(jnp.concatenate([hi, lo], axis=1), rhs, preferred_element_type=f32)


def _key_offset(tq, tk):
    return lax.broadcasted_iota(jnp.int32, (tq, tk), 1) - lax.broadcasted_iota(jnp.int32, (tq, tk), 0)


def _attn_geometry(T, tq, tk, nsub):
    tq = _row_tile(T, tq)
    nsub = max(1, min(nsub, T // tq))
    while T % (nsub * tq):
        nsub -= 1
    return tq, min(tk, tq), nsub


def _chain_block(it, nkb, tk):
    return it < nkb, pl.multiple_of(jnp.maximum(nkb - 1 - it, 0) * tk, tk)


def _sweep_live(it, rs, nkb):
    m = None
    for r, n in zip(rs, nkb):
        ra = jnp.where(it < n, r, 2.0 * EXP_UNDERFLOW)
        m = ra if m is None else jnp.maximum(m, ra)
    return jnp.max(m) > EXP_UNDERFLOW


def attn_fwd(qkv, name, exch=None, tq=256, tk=128, nsub=4):
    T, D3 = qkv.shape
    D = D3 // 3
    H = D // HEAD_DIM
    tq, tk, nsub = _attn_geometry(T, tq, tk, nsub)
    bq = nsub * tq
    scale = HEAD_DIM ** -0.5

    never = -(tq + tk)
    R = range(nsub)

    def body(q_ref, k_ref, v_ref, o_ref, of_ref):
        i = pl.program_id(1)
        rhs = _sum_rhs(tk, False)
        cmr = _key_offset(tq, tk)
        nkb = [(i * nsub + a + 1) * (tq // tk) for a in R]
        qs = [q_ref[a * tq:(a + 1) * tq, :] for a in R]

        def cond(c):
            return _sweep_live(c[0], c[1], nkb)

        def step(c):
            it, rs, accs, fines = c
            vk = [_chain_block(it, nkb[a], tk) for a in R]
            k0 = [vk[a][1] for a in R]
            off = (it + 1) * tk - tq
            zs = [lax.dot_general(qs[a], k_ref[pl.ds(k0[a], tk), :], NT_DIMS, preferred_element_type=f32) * scale for a in R]
            masks = [cmr < jnp.where(vk[a][0], off, never) for a in R]
            lsms = [jnp.where(masks[a], jnp.minimum(-zs[a], 0.0) - jnp.log(1.0 + jnp.exp(-jnp.abs(zs[a]))), 0.0) for a in R]
            lsum = [_split_sums(lsms[a], rhs) for a in R]
            ws = [jnp.where(masks[a], jnp.exp(zs[a] + lsum[a][:, :tk] + rs[a]), 0.0) for a in R]
            his = [ws[a].astype(bf16) for a in R]
            los = [(ws[a] - his[a].astype(f32)).astype(bf16) for a in R]
            accs2 = [accs[a] + jnp.dot(his[a], v_ref[pl.ds(k0[a], tk), :], preferred_element_type=f32) for a in R]
            fines2 = [fines[a] + jnp.dot(los[a], v_ref[pl.ds(k0[a], tk), :], preferred_element_type=f32) for a in R]
            return it + 1, [rs[a] + lsum[a][:, tk:] for a in R], accs2, fines2

        zeros = [jnp.zeros((tq, HEAD_DIM), f32)] * nsub
        _, _, accs, fines = lax.while_loop(cond, step, (jnp.int32(0), [jnp.zeros((tq, tk), f32)] * nsub, zeros, zeros))
        for a in R:
            o_ref[a * tq:(a + 1) * tq, :] = accs[a].astype(bf16)
            of_ref[a * tq:(a + 1) * tq, :] = accs[a] + fines[a]

    qblk = pl.BlockSpec((bq, HEAD_DIM), lambda h, i: (i, h))
    return _run(
        body, (qkv, qkv, qkv), exch, name=name, grid=(H, T // bq),
        in_specs=[qblk, pl.BlockSpec((T, HEAD_DIM), lambda h, i: (0, H + h)),
                  pl.BlockSpec((T, HEAD_DIM), lambda h, i: (0, 2 * H + h))],
        out_specs=[qblk, qblk], out_shape=[jax.ShapeDtypeStruct((T, D), bf16), jax.ShapeDtypeStruct((T, D), f32)],
        sem=("parallel", "arbitrary"))


def attn_bwd(qkv, do, o_fine, name, exch=None, tq=256, tk=128, nsub=4):
    T, D3 = qkv.shape
    D = D3 // 3
    H = D // HEAD_DIM
    tq, tk, nsub = _attn_geometry(T, tq, tk, nsub)
    assert tk == HEAD_DIM
    bq = nsub * tq
    nq = T // bq
    scale = HEAD_DIM ** -0.5
    never = -(tq + tk)
    R = range(nsub)

    def body(q_ref, k_ref, v_ref, do_ref, of_ref, dq_ref, dk_ref, dv_ref, dk_acc, dv_acc):
        i = pl.program_id(1)

        @pl.when(i == 0)
        def _():
            dk_acc[...] = jnp.zeros_like(dk_acc)
            dv_acc[...] = jnp.zeros_like(dv_acc)

        rhs_incl, rhs_excl = _sum_rhs(tk, False), _sum_rhs(tk, True)
        cmr = _key_offset(tq, tk)
        nkb = [(i * nsub + a + 1) * (tq // tk) for a in R]
        qs = [q_ref[a * tq:(a + 1) * tq, :] for a in R]
        dos = [do_ref[a * tq:(a + 1) * tq, :] for a in R]
        zeros = [jnp.zeros((tq, tk), f32)] * nsub

        def cond(c):
            return _sweep_live(c[0], c[1], nkb)

        def weights(it, rs):
            vk = [_chain_block(it, nkb[a], tk) for a in R]
            k0 = [vk[a][1] for a in R]
            off = (it + 1) * tk - tq
            zs = [lax.dot_general(qs[a], k_ref[pl.ds(k0[a], tk), :], NT_DIMS, preferred_element_type=f32) * scale for a in R]
            das = [lax.dot_general(dos[a], v_ref[pl.ds(k0[a], tk), :], NT_DIMS, preferred_element_type=f32) for a in R]
            masks = [cmr < jnp.where(vk[a][0], off, never) for a in R]
            es = [jnp.exp(-jnp.abs(zs[a])) for a in R]
            lsms = [jnp.where(masks[a], jnp.minimum(-zs[a], 0.0) - jnp.log(1.0 + es[a]), 0.0) for a in R]
            lsum = [_split_sums(lsms[a], rhs_incl) for a in R]
            ws = [jnp.where(masks[a], jnp.exp(zs[a] + lsum[a][:, :tk] + rs[a]), 0.0) for a in R]
            gs = [ws[a] * das[a] for a in R]
            gsum = [_split_sums(gs[a], rhs_excl) for a in R]
            return k0, masks, zs, es, ws, gs, lsum, gsum

        g_total = [_split_sums(dos[a].astype(f32) * of_ref[a * tq:(a + 1) * tq, :], rhs_incl)[:, tk:] for a in R]

        def grad_step(c):
            it, rs, runs, dqs = c
            k0, masks, zs, es, ws, gs, lsum, gsum = weights(it, rs)
            sig = [jnp.where(zs[a] >= 0.0, 1.0, es[a]) / (1.0 + es[a]) for a in R]
            prefix = [g_total[a] - (runs[a] + gsum[a][:, :tk]) for a in R]
            dzs = [(jnp.where(masks[a], gs[a] - sig[a] * prefix[a], 0.0) * scale).astype(bf16) for a in R]
            wbs = [ws[a].astype(bf16) for a in R]
            dqs2 = [dqs[a] + jnp.dot(dzs[a], k_ref[pl.ds(k0[a], tk), :], preferred_element_type=f32) for a in R]
            dks = [lax.dot_general(dzs[a], qs[a], TN_DIMS, preferred_element_type=f32) for a in R]
            dvs = [lax.dot_general(wbs[a], dos[a], TN_DIMS, preferred_element_type=f32) for a in R]
            for a in R:
                dk_acc[pl.ds(k0[a], tk), :] += dks[a]
                dv_acc[pl.ds(k0[a], tk), :] += dvs[a]
            return it + 1, [rs[a] + lsum[a][:, tk:] for a in R], [runs[a] + gsum[a][:, tk:] for a in R], dqs2

        init = (jnp.int32(0), zeros, zeros, [jnp.zeros((tq, HEAD_DIM), f32)] * nsub)
        _, _, _, dqs = lax.while_loop(cond, grad_step, init)
        for a in R:
            dq_ref[a * tq:(a + 1) * tq, :] = dqs[a].astype(bf16)

        @pl.when(i == nq - 1)
        def _():
            dk_ref[...] = dk_acc[...].astype(bf16)
            dv_ref[...] = dv_acc[...].astype(bf16)

    qblk = pl.BlockSpec((bq, HEAD_DIM), lambda h, i: (i, h))
    head = pl.BlockSpec((T, HEAD_DIM), lambda h, i: (0, h))
    return _run(
        body, (qkv, qkv, qkv, do, o_fine), exch, name=name, grid=(H, nq),
        in_specs=[qblk, pl.BlockSpec((T, HEAD_DIM), lambda h, i: (0, H + h)),
                  pl.BlockSpec((T, HEAD_DIM), lambda h, i: (0, 2 * H + h)), qblk, qblk],
        out_specs=[qblk, head, head], out_shape=[jax.ShapeDtypeStruct((T, D), bf16)] * 3,
        scratch_shapes=[pltpu.VMEM((T, HEAD_DIM), f32), pltpu.VMEM((T, HEAD_DIM), f32)],
        sem=("parallel", "arbitrary"))


def sum_partials(p, name, tr=256):
    K, R, C = p.shape
    tr = _row_tile(R, tr)

    def body(p_ref, o_ref):
        g = p_ref[0].astype(f32)
        for k in range(1, K):
            g = g + p_ref[k].astype(f32)
        o_ref[...] = g

    return _pcall(
        body, name=name, grid=(R // tr,), in_specs=[pl.BlockSpec((K, tr, C), lambda i: (0, i, 0))],
        out_specs=pl.BlockSpec((tr, C), lambda i: (i, 0)), out_shape=jax.ShapeDtypeStruct((R, C), f32),
        compiler_params=_cp("parallel"))(p)


def adamw(p, w, m, v, slot, outs, name):
    K, R, C = p.shape
    L = w.shape[0]
    tc = _lane_tile(C, 1024)
    row_bytes = 2 * tc * (K * p.dtype.itemsize + 7 * 4)
    tr = _row_tile(R, max(8, MM_VMEM_BUDGET_BYTES // row_bytes // 8 * 8))
    c1 = 1.0 - ADAM_B1 ** ADAM_STEP
    c2 = 1.0 - ADAM_B2 ** ADAM_STEP
    n_old = 0 if outs is None else 4

    def body(p_ref, w_ref, m_ref, v_ref, *refs):
        g_ref, d_ref, nm_ref, nv_ref = refs[n_old:]
        g = p_ref[0].astype(f32)
        for k in range(1, K):
            g = g + p_ref[k].astype(f32)
        nm = ADAM_B1 * m_ref[...] + (1.0 - ADAM_B1) * g
        nv = ADAM_B2 * v_ref[...] + (1.0 - ADAM_B2) * (g * g)
        g_ref[...] = g
        nm_ref[...] = nm
        nv_ref[...] = nv
        d_ref[...] = -ADAM_LR * ((nm / c1) / (jnp.sqrt(nv / c2) + ADAM_EPS) + ADAM_WD * w_ref[...])

    blk = pl.BlockSpec((None, tr, tc), lambda i, j: (slot, i, j))
    old = [] if outs is None else list(outs)
    return _pcall(
        body, name=name, grid=(R // tr, C // tc),
        in_specs=[pl.BlockSpec((K, tr, tc), lambda i, j: (0, i, j)), blk, blk, blk] + [pl.BlockSpec(memory_space=pl.ANY)] * n_old,
        out_specs=[blk] * 4, out_shape=[jax.ShapeDtypeStruct((L, R, C), f32)] * 4,
        input_output_aliases={4 + t: t for t in range(n_old)}, compiler_params=_cp("parallel", "parallel"))(p, w, m, v, *old)


def _pack(parts):
    flat, slices, off = [], [], 0
    for a in parts:
        nel = a.size
        pad = -nel % PACK_ALIGN
        flat.append(jnp.pad(a.reshape(-1).astype(f32), (0, pad)))
        slices.append((off, nel, a.shape))
        off += nel + pad
    return jnp.concatenate(flat).reshape(-1, 128), slices


def _unpack(packed, slices):
    flat = packed.reshape(-1)
    return [flat[off:off + nel].reshape(shape) for off, nel, shape in slices]


def kernel(x, pool_w, pool_scale, attn_w_qkv, attn_w_o, ffn_w_up, ffn_conv_w, ffn_conv_b, ffn_w_down, ln_mix_g, ln_mix_b, ln_ffn_g, ln_ffn_b, loss_target, m_pool_w, m_pool_scale, m_attn_w_qkv, m_attn_w_o, m_ffn_w_up, m_ffn_conv_w, m_ffn_conv_b, m_ffn_w_down, m_ln_mix_g, m_ln_mix_b, m_ln_ffn_g, m_ln_ffn_b, v_pool_w, v_pool_scale, v_attn_w_qkv, v_attn_w_o, v_ffn_w_up, v_ffn_conv_w, v_ffn_conv_b, v_ffn_w_down, v_ln_mix_g, v_ln_mix_b, v_ln_ffn_g, v_ln_ffn_b):
    _, T, D = x.shape
    depth = ln_mix_g.shape[0]
    alpha = (2.0 * depth) ** 0.25
    G, CS, C = pool_w.shape[1:]
    FB = ffn_w_up.shape[2]
    JH = N_DEV // 2
    me = 4 * lax.axis_index("x") + 2 * lax.axis_index("y") + lax.axis_index("c")

    up_t, m_up_t, v_up_t = (jnp.swapaxes(a, 1, 2) for a in (ffn_w_up, m_ffn_w_up, v_ffn_w_up))
    shards = {}
    for l in range(depth):
        j = l // 2
        shards["up", l], shards["down", l] = up_t[l].astype(bf16), ffn_w_down[l].astype(bf16)
        if l % 2 == 0:
            shards["pool", l] = pool_w[j].astype(bf16)
        else:
            shards["qkv", l], shards["wo", l] = attn_w_qkv[j].astype(bf16), attn_w_o[j].astype(bf16)
    full = {}

    def gather_in(call, keys):
        keys = [k for k in keys if k[1] < depth]
        if not keys:
            return call(None)
        res, got = call(([shards[k] for k in keys], False))
        full.update(zip(keys, got))
        return res

    first = [("pool", 0), ("up", 0)]
    got = all_gather([shards[k] for k in first] + [ffn_conv_w], "gather_first")
    full.update(zip(first, got))
    cw_all = got[-1]

    xf = x[0]
    xb = xf.astype(bf16)
    saved, weights = [], []
    for l in range(depth):
        j, sv, wl = l // 2, {}, {}
        if l % 2 == 0:
            wl["pool"] = full["pool", l].transpose(1, 0, 2, 3).reshape(G, C, C)
            u, sv["pooled"] = pool_fwd(xf, wl["pool"], pool_scale[j], f"pool_fwd_l{l}")
            up_rides = ([("down", l)] if l == 0 else [("qkv", l + 1)]) + [("wo", l + 1)]
            down_rides = [("qkv", l + 1)] if l == 0 else []
        else:
            wl["qkv"], wl["wo"] = full["qkv", l], full["wo", l].reshape(1, D, D)
            sv["xin_b"] = xb
            sv["qkv"] = mm_nn(xb, wl["qkv"], bf16, f"qkv_l{l}", out_cols=True)
            sv["o"], sv["o_fine"] = gather_in(lambda e: attn_fwd(sv["qkv"], f"attn_fwd_l{l}", exch=e),
                                [("up", l), ("down", l), ("pool", l + 1)])
            u = mm_nn(sv["o"], wl["wo"], f32, f"attn_out_l{l}")[0]
            up_rides, down_rides = [("up", l + 1)], [("down", l + 1)]
        x1, sv["x1_b"], sv["xhat1"], sv["rstd1"] = ln_fwd(xf, u, ln_mix_g[l], ln_mix_b[l], alpha, f"ln_mix_l{l}")
        wl["up"], wl["cw"], wl["cb"] = full["up", l], cw_all[:, l], ffn_conv_b[l].reshape(N_DEV, 1, FB)
        sv["h"], sv["c"], sv["a"] = gather_in(
            lambda e: ffn_up_gate(sv["x1_b"], wl["up"], wl["cw"], wl["cb"], f"ffn_up_l{l}", exch=e), up_rides)
        wl["down"] = full["down", l].reshape(JH, FB, D)
        f = gather_in(lambda e: mm_nn_sum(sv["a"], wl["down"], f"ffn_down_l{l}", exch=e), down_rides)
        xf, xb, sv["xhat2"], sv["rstd2"] = ln_fwd(x1, f, ln_ffn_g[l], ln_ffn_b[l], alpha, f"ln_ffn_l{l}")
        saved.append(sv)
        weights.append(wl)

    dx, loss_row = loss_head(xf, loss_target[0], "loss_head")

    landed = {}

    def scatter_in(call, parts):
        res, got = call(([p for _, p in parts], True))
        landed.update(zip([k for k, _ in parts], got))
        return res

    grads = [None] * depth
    res, pending = None, []
    for l in reversed(range(depth)):
        wl, sv, j, gl = weights[l], saved[l], l // 2, {}
        ds2, ds2_b, gl["ln_ffn_g"], gl["ln_ffn_b"] = ln_bwd(dx, res, alpha, sv["xhat2"], sv["rstd2"], ln_ffn_g[l], f"ln_ffn_bwd_l{l}")
        g_down = mm_tn(sv["a"], ds2_b[None], bf16, f"ffn_down_dw_l{l}").reshape(N_DEV, FB // 2, D)
        dh, gl["cw"], gl["cb"] = ffn_down_gate_bwd(ds2_b, wl["down"], sv["h"], sv["c"], wl["cw"], f"ffn_down_dx_l{l}")
        g_up = scatter_in(lambda e: mm_tn(dh, sv["x1_b"][None], bf16, f"ffn_up_dw_l{l}", exch=e),
                          [(("down", l), g_down)] + pending)
        dx1 = scatter_in(lambda e: mm_nn_sum(dh, wl["up"], f"ffn_up_dx_l{l}", exch=e), [(("up", l), g_up)])
        ds1, ds1_b, gl["ln_mix_g"], gl["ln_mix_b"] = ln_bwd(dx1, ds2, alpha, sv["xhat1"], sv["rstd1"], ln_mix_g[l], f"ln_mix_bwd_l{l}")
        if l % 2 == 0:
            dx, g_pool, gl["pool_scale"] = pool_bwd(ds1, sv["pooled"], wl["pool"], pool_scale[j], alpha, f"pool_bwd_l{l}")
            res, pending = None, [(("pool", l), g_pool.reshape(G, N_DEV, CS, C).transpose(1, 0, 2, 3).astype(bf16))]
        else:
            do = mm_nt(ds1_b, wl["wo"], bf16, f"attn_out_dx_l{l}")[0]
            g_wo = mm_tn(sv["o"][None], ds1_b[None], bf16, f"attn_out_dw_l{l}").reshape(N_DEV, D // N_DEV, D)
            dqkv = jnp.concatenate(scatter_in(lambda e: attn_bwd(sv["qkv"], do, sv["o_fine"], f"attn_bwd_l{l}", exch=e),
                                              [(("wo", l), g_wo)]), axis=1)
            g_qkv = mm_tn(sv["xin_b"][None], dqkv, bf16, f"qkv_dw_l{l}", b_cols=N_DEV)
            dx = scatter_in(lambda e: mm_nt_sum(dqkv, wl["qkv"], f"qkv_dx_l{l}", a_cols=True, exch=e), [(("qkv", l), g_qkv)])
            res, pending = ds1, []
        grads[l] = gl
    assert res is None, "the first layer is a pooling layer: its backward adds the residual path itself"
    grad_x = dx[None]
    if pending:
        landed.update(zip([k for k, _ in pending], scatter_partials([p for _, p in pending], "scatter_last")))

    def stack_update(name, kind, w, m, v):
        shape = w.shape
        rows = w[0].size // shape[-1]
        flat = [a.reshape(shape[0], rows, shape[-1]) for a in (w, m, v)]
        outs = None
        for i, l in enumerate(sorted(l for k, l in landed if k == kind)):
            outs = adamw(landed[kind, l].reshape(N_DEV, rows, shape[-1]), *flat, i, outs, f"adamw_{name}_{i}")
        return [o.reshape(shape) for o in outs]

    big = {
        "pool_w": stack_update("pool_w", "pool", pool_w, m_pool_w, v_pool_w),
        "attn_w_qkv": stack_update("attn_w_qkv", "qkv", attn_w_qkv, m_attn_w_qkv, v_attn_w_qkv),
        "attn_w_o": stack_update("attn_w_o", "wo", attn_w_o, m_attn_w_o, v_attn_w_o),
        "ffn_w_up": [jnp.swapaxes(o, 1, 2) for o in stack_update("ffn_w_up", "up", up_t, m_up_t, v_up_t)],
        "ffn_w_down": stack_update("ffn_w_down", "down", ffn_w_down, m_ffn_w_down, v_ffn_w_down),
    }

    def per_layer(key):
        return jnp.stack([grads[l][key].reshape(-1) for l in range(depth)])

    small_parts = [per_layer("ln_mix_g"), per_layer("ln_mix_b"), per_layer("ln_ffn_g"), per_layer("ln_ffn_b"),
                   jnp.stack([grads[l]["pool_scale"].reshape(-1) for l in range(0, depth, 2)]),
                   per_layer("cb"), jnp.stack([grads[l]["cw"] for l in range(depth)]), loss_row[0, :1]]
    packed, slices = _pack(small_parts)
    total = sum_partials(all_gather([packed], "gather_small")[0], "sum_small")
    g_mix_g, g_mix_b, g_ffn_g, g_ffn_b, g_scale, g_cb, g_cw_all, loss = _unpack(total, slices)
    g_cw = lax.dynamic_index_in_dim(g_cw_all, me, axis=1, keepdims=False)

    small_w = [ln_mix_g, ln_mix_b, ln_ffn_g, ln_ffn_b, pool_scale, ffn_conv_b, ffn_conv_w]
    small_m = [m_ln_mix_g, m_ln_mix_b, m_ln_ffn_g, m_ln_ffn_b, m_pool_scale, m_ffn_conv_b, m_ffn_conv_w]
    small_v = [v_ln_mix_g, v_ln_mix_b, v_ln_ffn_g, v_ln_ffn_b, v_pool_scale, v_ffn_conv_b, v_ffn_conv_w]
    small_g = [g_mix_g, g_mix_b, g_ffn_g, g_ffn_b, g_scale, g_cb.reshape(ffn_conv_b.shape), g_cw]
    pw, wslices = _pack(small_w)
    res = adamw(_pack(small_g)[0][None], pw[None], _pack(small_m)[0][None], _pack(small_v)[0][None], 0, None, "adamw_small")
    small = dict(zip(["ln_mix_g", "ln_mix_b", "ln_ffn_g", "ln_ffn_b", "pool_scale", "ffn_conv_b", "ffn_conv_w"],
                     zip(*[_unpack(r[0], wslices) for r in res])))

    order = ["pool_w", "pool_scale", "attn_w_qkv", "attn_w_o", "ffn_w_up", "ffn_conv_w", "ffn_conv_b", "ffn_w_down",
             "ln_mix_g", "ln_mix_b", "ln_ffn_g", "ln_ffn_b"]
    table = {**big, **{k: list(val) for k, val in small.items()}}
    outs = [loss.reshape(()), grad_x]
    for t in range(4):
        outs += [table[name][t] for name in order]
    return tuple(outs)
```

```python
import jax
import jax.numpy as jnp
from jax import lax
from jax.experimental import pallas as pl
from jax.experimental.pallas import tpu as pltpu

f32, bf16 = jnp.float32, jnp.bfloat16

N_DEV = 8
HEAD_DIM = 128
POOL_WINDOWS = (2, 4, 8, 16)
POOL_HALO = 16
CONV_HALO = 8
LN_EPS = 1e-5
ADAM_LR, ADAM_B1, ADAM_B2, ADAM_EPS, ADAM_WD, ADAM_STEP = 0.001, 0.9, 0.999, 1e-08, 0.01, 10
V7X_VMEM_BYTES = 64 * 1024 * 1024
VMEM_LIMIT_BYTES = V7X_VMEM_BYTES - 8 * 1024 * 1024
MM_VMEM_BUDGET_BYTES = 40 * 1024 * 1024
EXP_UNDERFLOW = -104.0
PACK_ALIGN = 8 * 128

NT_DIMS = (((1,), (1,)), ((), ()))
TN_DIMS = (((0,), (0,)), ((), ()))


def _pcall(body, **kw):
    return pl.pallas_call(body, **kw)


def _cp(*sem):
    return pltpu.CompilerParams(dimension_semantics=sem, vmem_limit_bytes=VMEM_LIMIT_BYTES)


def _row_tile(rows, cap):
    if rows <= cap:
        return rows
    best = None
    for t in range(8, cap + 1, 8):
        if rows % t == 0:
            best = t
    assert best is not None, rows
    return best


def _lane_tile(cols, cap):
    best = cols
    for t in range(128, min(cap, cols) + 1, 128):
        if cols % t == 0:
            best = t
    return best if cols > cap else cols


N_PEER = N_DEV - 1


def _mesh_place():
    x, y, c = lax.axis_index("x"), lax.axis_index("y"), lax.axis_index("c")
    peers = []
    for k in range(1, N_DEV):
        px = 1 - x if k & 4 else x
        py = 1 - y if k & 2 else y
        pc = 1 - c if k & 1 else c
        peers.append(((px, py, pc), 4 * px + 2 * py + pc))
    return 4 * x + 2 * y + c, peers


def _exchange_copies(ins, outs, send, recv, loc, place, scatter):
    me, peers = place

    def src(a, idx):
        return ins[a].at[idx] if scatter else ins[a]

    local, starts, waits = [], [], []
    for a in range(len(ins)):
        local.append(pltpu.make_async_copy(src(a, me), outs[a].at[me], loc.at[a]))
        for k, (dev, idx) in enumerate(peers):
            sems = dict(send_sem=send.at[a * N_PEER + k], recv_sem=recv.at[a * N_PEER + k],
                        device_id=dev, device_id_type=pl.DeviceIdType.MESH)
            starts.append(pltpu.make_async_remote_copy(src_ref=src(a, idx), dst_ref=outs[a].at[me], **sems))
            waits.append(pltpu.make_async_remote_copy(src_ref=src(a, idx), dst_ref=outs[a].at[idx], **sems))
    return local, starts, waits


def _exchange_shapes(arrs, scatter):
    return [jax.ShapeDtypeStruct(a.shape if scatter else (N_DEV, *a.shape), a.dtype) for a in arrs]


def _exchange_sems(n):
    return [pltpu.SemaphoreType.DMA((n * N_PEER,)), pltpu.SemaphoreType.DMA((n * N_PEER,)), pltpu.SemaphoreType.DMA((n,))]


def _exchange(arrs, scatter, name):
    n = len(arrs)

    def body(*refs):
        local, starts, waits = _exchange_copies(refs[:n], refs[n:2 * n], *refs[2 * n:], _mesh_place(), scatter)
        for cp in local + starts:
            cp.start()
        for cp in waits + local:
            cp.wait()

    any_spec = pl.BlockSpec(memory_space=pl.ANY)
    return _pcall(body, name=name, in_specs=[any_spec] * n, out_specs=[any_spec] * n,
                  out_shape=_exchange_shapes(arrs, scatter), scratch_shapes=_exchange_sems(n))(*arrs)


def all_gather(shards, name):
    return _exchange(shards, False, name)


def scatter_partials(partials, name):
    return _exchange(partials, True, name)


def _run(body, args, exch, *, name, grid, in_specs, out_specs, out_shape, scratch_shapes=(), sem):
    scratch_shapes = list(scratch_shapes)
    if exch is None:
        return _pcall(body, name=name, grid=grid, in_specs=in_specs, out_specs=out_specs, out_shape=out_shape,
                      scratch_shapes=scratch_shapes, compiler_params=_cp(*sem))(*args)
    arrs, scatter = exch
    single = not isinstance(out_shape, (list, tuple))
    out_specs, out_shape = ([out_specs], [out_shape]) if single else (list(out_specs), list(out_shape))
    n, na, nb, ns = len(arrs), len(in_specs), len(out_specs), len(scratch_shapes)

    def hosted(*refs):
        ins, xin = refs[:na], refs[na:na + n]
        outs, xout = refs[na + n:na + n + nb], refs[na + n + nb:na + 2 * n + nb]
        scr, sems = refs[na + 2 * n + nb:na + 2 * n + nb + ns], refs[na + 2 * n + nb + ns:]
        first, last = None, None
        for d, size in enumerate(grid):
            pid = pl.program_id(d)
            first = (pid == 0) if first is None else jnp.logical_and(first, pid == 0)
            last = (pid == size - 1) if last is None else jnp.logical_and(last, pid == size - 1)
        local, starts, waits = _exchange_copies(xin, xout, *sems, _mesh_place(), scatter)

        @pl.when(first)
        def _():
            for cp in local + starts:
                cp.start()

        body(*ins, *outs, *scr)

        @pl.when(last)
        def _():
            for cp in waits + local:
                cp.wait()

    any_spec = pl.BlockSpec(memory_space=pl.ANY)
    res = _pcall(hosted, name=name, grid=grid, in_specs=list(in_specs) + [any_spec] * n,
                 out_specs=out_specs + [any_spec] * n, out_shape=out_shape + _exchange_shapes(arrs, scatter),
                 scratch_shapes=scratch_shapes + _exchange_sems(n),
                 compiler_params=_cp(*(["arbitrary"] * len(grid))))(*args, *arrs)
    return (res[0] if single else list(res[:nb])), list(res[nb:])


def mm_nn(a, b, out_dtype, name, out_cols=False, exch=None, tm=512):
    M, K = a.shape
    J, _, N = b.shape
    tm = _row_tile(M, tm)

    def body(a_ref, b_ref, o_ref):
        o_ref[...] = jnp.dot(a_ref[...], b_ref[...], preferred_element_type=f32).astype(o_ref.dtype)

    if out_cols:
        out_spec, out_shape = pl.BlockSpec((tm, N), lambda j, i: (i, j)), (M, J * N)
    else:
        out_spec, out_shape = pl.BlockSpec((None, tm, N), lambda j, i: (j, i, 0)), (J, M, N)
    return _run(
        body, (a, b), exch, name=name, grid=(J, M // tm),
        in_specs=[pl.BlockSpec((tm, K), lambda j, i: (i, 0)), pl.BlockSpec((None, K, N), lambda j, i: (j, 0, 0))],
        out_specs=out_spec, out_shape=jax.ShapeDtypeStruct(out_shape, out_dtype), sem=("parallel", "parallel"))


def mm_nn_sum(a, b, name, exch=None, tm=1024):
    J, M, K = a.shape
    N = b.shape[2]
    tm = _row_tile(M, tm)

    def body(a_ref, b_ref, o_ref):
        j = pl.program_id(1)
        p = jnp.dot(a_ref[...], b_ref[...], preferred_element_type=f32)

        @pl.when(j == 0)
        def _():
            o_ref[...] = p

        @pl.when(j > 0)
        def _():
            o_ref[...] += p

    return _run(
        body, (a, b), exch, name=name, grid=(M // tm, J),
        in_specs=[pl.BlockSpec((None, tm, K), lambda i, j: (j, i, 0)), pl.BlockSpec((None, K, N), lambda i, j: (j, 0, 0))],
        out_specs=pl.BlockSpec((tm, N), lambda i, j: (i, 0)), out_shape=jax.ShapeDtypeStruct((M, N), f32),
        sem=("parallel", "arbitrary"))


def mm_nt(a, b, out_dtype, name, exch=None, tm=512):
    M, K = a.shape
    J, N, _ = b.shape
    tm = _row_tile(M, tm)

    def body(a_ref, b_ref, o_ref):
        o_ref[...] = lax.dot_general(a_ref[...], b_ref[...], NT_DIMS, preferred_element_type=f32).astype(o_ref.dtype)

    return _run(
        body, (a, b), exch, name=name, grid=(J, M // tm),
        in_specs=[pl.BlockSpec((tm, K), lambda j, i: (i, 0)), pl.BlockSpec((None, N, K), lambda j, i: (j, 0, 0))],
        out_specs=pl.BlockSpec((None, tm, N), lambda j, i: (j, i, 0)), out_shape=jax.ShapeDtypeStruct((J, M, N), out_dtype),
        sem=("parallel", "parallel"))


def mm_nt_sum(a, b, name, a_cols=False, exch=None, tm=1024):
    J, N, K = b.shape
    M = a.shape[0] if a_cols else a.shape[1]
    tm = _row_tile(M, tm)

    def body(a_ref, b_ref, o_ref):
        j = pl.program_id(1)
        p = lax.dot_general(a_ref[...], b_ref[...], NT_DIMS, preferred_element_type=f32)

        @pl.when(j == 0)
        def _():
            o_ref[...] = p

        @pl.when(j > 0)
        def _():
            o_ref[...] += p

    a_spec = pl.BlockSpec((tm, K), lambda i, j: (i, j)) if a_cols else pl.BlockSpec((None, tm, K), lambda i, j: (j, i, 0))
    return _run(
        body, (a, b), exch, name=name, grid=(M // tm, J),
        in_specs=[a_spec, pl.BlockSpec((None, N, K), lambda i, j: (j, 0, 0))],
        out_specs=pl.BlockSpec((tm, N), lambda i, j: (i, 0)), out_shape=jax.ShapeDtypeStruct((M, N), f32),
        sem=("parallel", "arbitrary"))


def mm_tn(a, b, out_dtype, name, b_cols=0, exch=None, tm=1024, tn=1024, tk=2048):
    JA, T, M = a.shape
    if b_cols:
        JB, N = b_cols, b.shape[1] // b_cols
    else:
        JB, _, N = b.shape
    J = max(JA, JB)
    tm = _lane_tile(M, tm)
    tn = _lane_tile(N, tn)
    tk = _row_tile(T, tk)

    def vmem_bytes(tk):
        return 2 * 2 * tk * (tm + tn) + 2 * 4 * tm * tn + 2 * jnp.dtype(out_dtype).itemsize * tm * tn

    while vmem_bytes(tk) > MM_VMEM_BUDGET_BYTES and tk % 16 == 0:
        tk //= 2
    nk, nn = T // tk, N // tn

    def body(a_ref, b_ref, o_ref, acc):
        k = pl.program_id(3)
        p = lax.dot_general(a_ref[...], b_ref[...], TN_DIMS, preferred_element_type=f32)

        @pl.when(k == 0)
        def _():
            acc[...] = p

        @pl.when(k > 0)
        def _():
            acc[...] += p

        @pl.when(k == nk - 1)
        def _():
            o_ref[...] = acc[...].astype(o_ref.dtype)

    a_spec = pl.BlockSpec((None, tk, tm), (lambda j, i, n, k: (j, k, i)) if JA > 1 else (lambda j, i, n, k: (0, k, i)))
    if b_cols:
        b_spec = pl.BlockSpec((tk, tn), lambda j, i, n, k: (k, j * nn + n))
    else:
        b_spec = pl.BlockSpec((None, tk, tn), (lambda j, i, n, k: (j, k, n)) if JB > 1 else (lambda j, i, n, k: (0, k, n)))
    return _run(
        body, (a, b), exch, name=name, grid=(J, M // tm, nn, nk), in_specs=[a_spec, b_spec],
        out_specs=pl.BlockSpec((None, tm, tn), lambda j, i, n, k: (j, i, n)),
        out_shape=jax.ShapeDtypeStruct((J, M, N), out_dtype), scratch_shapes=[pltpu.VMEM((tm, tn), f32)],
        sem=("parallel", "parallel", "parallel", "arbitrary"))


def ln_fwd(x, u, g, b, alpha, name, tm=256):
    T, D = x.shape
    tm = _row_tile(T, tm)

    def body(x_ref, u_ref, g_ref, b_ref, y_ref, yb_ref, xh_ref, rs_ref):
        s = alpha * x_ref[...] + u_ref[...]
        mu = jnp.mean(s, axis=-1, keepdims=True)
        c = s - mu
        var = jnp.mean(c * c, axis=-1, keepdims=True)
        r = lax.rsqrt(var + LN_EPS)
        xh = c * r
        y = xh * g_ref[...] + b_ref[...]
        y_ref[...] = y
        yb_ref[...] = y.astype(bf16)
        xh_ref[...] = xh
        rs_ref[...] = r

    row = pl.BlockSpec((tm, D), lambda i: (i, 0))
    vec = pl.BlockSpec((1, D), lambda i: (0, 0))
    return _pcall(
        body, name=name, grid=(T // tm,), in_specs=[row, row, vec, vec],
        out_specs=[row, row, row, pl.BlockSpec((tm, 1), lambda i: (i, 0))],
        out_shape=[jax.ShapeDtypeStruct((T, D), f32), jax.ShapeDtypeStruct((T, D), bf16),
                   jax.ShapeDtypeStruct((T, D), f32), jax.ShapeDtypeStruct((T, 1), f32)],
        compiler_params=_cp("parallel"))(x, u, g.reshape(1, D), b.reshape(1, D))


def ln_bwd(dy, res, alpha, xhat, rstd, g, name, tm=256):
    T, D = dy.shape
    tm = _row_tile(T, tm)
    n_dy = 1 if res is None else 2

    def body(*refs):
        dy_refs, (xh_ref, rs_ref, g_ref, ds_ref, dsb_ref, dg_ref, db_ref) = refs[:n_dy], refs[n_dy:]
        i = pl.program_id(0)
        dy_t, xh = dy_refs[0][...], xh_ref[...]
        if res is not None:
            dy_t = dy_t + alpha * dy_refs[1][...]
        dxh = dy_t * g_ref[...]
        m1 = jnp.mean(dxh, axis=-1, keepdims=True)
        m2 = jnp.mean(dxh * xh, axis=-1, keepdims=True)
        ds = rs_ref[...] * (dxh - m1 - xh * m2)
        ds_ref[...] = ds
        dsb_ref[...] = ds.astype(bf16)
        pg = jnp.sum(dy_t * xh, axis=0, keepdims=True)
        pb = jnp.sum(dy_t, axis=0, keepdims=True)

        @pl.when(i == 0)
        def _():
            dg_ref[...] = pg
            db_ref[...] = pb

        @pl.when(i > 0)
        def _():
            dg_ref[...] += pg
            db_ref[...] += pb

    row = pl.BlockSpec((tm, D), lambda i: (i, 0))
    vec = pl.BlockSpec((1, D), lambda i: (0, 0))
    dys = (dy,) if res is None else (dy, res)
    return _pcall(
        body, name=name, grid=(T // tm,), in_specs=[row] * n_dy + [row, pl.BlockSpec((tm, 1), lambda i: (i, 0)), vec],
        out_specs=[row, row, vec, vec],
        out_shape=[jax.ShapeDtypeStruct((T, D), f32), jax.ShapeDtypeStruct((T, D), bf16),
                   jax.ShapeDtypeStruct((1, D), f32), jax.ShapeDtypeStruct((1, D), f32)],
        compiler_params=_cp("arbitrary"))(*dys, xhat, rstd, g.reshape(1, D))


def loss_head(y, target, name, tm=256):
    T, D = y.shape
    tm = _row_tile(T, tm)
    nt = T // tm

    def body(y_ref, t_ref, dy_ref, l_ref, acc):
        i = pl.program_id(0)
        e = y_ref[...] - t_ref[...]
        dy_ref[...] = e / D
        p = jnp.sum(e * e, axis=0, keepdims=True)

        @pl.when(i == 0)
        def _():
            acc[...] = p

        @pl.when(i > 0)
        def _():
            acc[...] += p

        @pl.when(i == nt - 1)
        def _():
            l_ref[...] = jnp.full(l_ref.shape, 0.5 * jnp.sum(acc[...]) / D, f32)

    row = pl.BlockSpec((tm, D), lambda i: (i, 0))
    return _pcall(
        body, name=name, grid=(nt,), in_specs=[row, row],
        out_specs=[row, pl.BlockSpec((1, 128), lambda i: (0, 0))],
        out_shape=[jax.ShapeDtypeStruct((T, D), f32), jax.ShapeDtypeStruct((1, 128), f32)],
        scratch_shapes=[pltpu.VMEM((1, D), f32)], compiler_params=_cp("arbitrary"))(y, target)


def pool_fwd(x, w, scale, name, exch=None, tm=256):
    T, D = x.shape
    G, C, _ = w.shape
    tm = _row_tile(T, tm)
    assert all(wd & (wd - 1) == 0 and wd - 1 <= POOL_HALO for wd in POOL_WINDOWS) and tm >= POOL_HALO

    def body(x_ref, w_ref, s_ref, u_ref, p_ref, halo):
        i = pl.program_id(0)

        @pl.when(i == 0)
        def _():
            halo[...] = jnp.zeros_like(halo)

        cur = x_ref[...]
        cat = jnp.concatenate([halo[...], cur], axis=0)
        halo[...] = cur[tm - POOL_HALO:, :]
        t1 = i * tm + lax.broadcasted_iota(jnp.int32, (tm, 1), 0) + 1
        for gi, wd in enumerate(POOL_WINDOWS):
            lo, hi = gi * C, (gi + 1) * C
            win = cat[:, lo:hi]
            sh = 1
            while sh < wd:
                win = win + pltpu.roll(win, sh, 0)
                sh *= 2
            cnt = jnp.minimum(t1, wd).astype(f32)
            pooled = (win[POOL_HALO:, :] / cnt - cur[:, lo:hi]).astype(bf16)
            p_ref[:, lo:hi] = pooled
            u_ref[:, lo:hi] = jnp.dot(pooled, w_ref[gi], preferred_element_type=f32) * s_ref[:, lo:hi]

    row = pl.BlockSpec((tm, D), lambda i: (i, 0))
    return _run(
        body, (x, w, scale.reshape(1, D)), exch, name=name, grid=(T // tm,),
        in_specs=[row, pl.BlockSpec((G, C, C), lambda i: (0, 0, 0)), pl.BlockSpec((1, D), lambda i: (0, 0))],
        out_specs=[row, row], out_shape=[jax.ShapeDtypeStruct((T, D), f32), jax.ShapeDtypeStruct((T, D), bf16)],
        scratch_shapes=[pltpu.VMEM((POOL_HALO, D), f32)], sem=("arbitrary",))


def pool_bwd(du, pooled, w, scale, alpha, name, tm=256):
    T, D = du.shape
    G, C, _ = w.shape
    tm = _row_tile(T, tm)
    nt = T // tm
    n = tm + POOL_HALO

    def body(du_ref, p_ref, w_ref, s_ref, dx_ref, dw_ref, dsc_ref, halo):
        i = pl.program_id(0)

        @pl.when(i == 0)
        def _():
            halo[...] = jnp.zeros_like(halo)
            dw_ref[...] = jnp.zeros_like(dw_ref)
            dsc_ref[...] = jnp.zeros_like(dsc_ref)

        t1 = (nt - 1 - i) * tm + lax.broadcasted_iota(jnp.int32, (tm, 1), 0) + 1
        for gi, wd in enumerate(POOL_WINDOWS):
            lo, hi = gi * C, (gi + 1) * C
            du_g, pb = du_ref[:, lo:hi], p_ref[:, lo:hi]
            yg = jnp.dot(pb, w_ref[gi], preferred_element_type=f32)
            dsc_ref[:, lo:hi] += jnp.sum(du_g * yg, axis=0, keepdims=True)
            dyg = (du_g * s_ref[:, lo:hi]).astype(bf16)
            dw_ref[gi] += lax.dot_general(pb, dyg, TN_DIMS, preferred_element_type=f32)
            dp = lax.dot_general(dyg, w_ref[gi], NT_DIMS, preferred_element_type=f32)
            e = dp / jnp.minimum(t1, wd).astype(f32)
            win = jnp.concatenate([e, halo[:, lo:hi]], axis=0)
            halo[:, lo:hi] = e[:POOL_HALO, :]
            sh = 1
            while sh < wd:
                win = win + pltpu.roll(win, n - sh, 0)
                sh *= 2
            dx_ref[:, lo:hi] = alpha * du_g + win[:tm, :] - dp

    row = pl.BlockSpec((tm, D), lambda i: (nt - 1 - i, 0))
    return _pcall(
        body, name=name, grid=(nt,),
        in_specs=[row, row, pl.BlockSpec((G, C, C), lambda i: (0, 0, 0)), pl.BlockSpec((1, D), lambda i: (0, 0))],
        out_specs=[row, pl.BlockSpec((G, C, C), lambda i: (0, 0, 0)), pl.BlockSpec((1, D), lambda i: (0, 0))],
        out_shape=[jax.ShapeDtypeStruct((T, D), f32), jax.ShapeDtypeStruct((G, C, C), f32), jax.ShapeDtypeStruct((1, D), f32)],
        scratch_shapes=[pltpu.VMEM((POOL_HALO, D), f32)], compiler_params=_cp("arbitrary"))(du, pooled, w, scale.reshape(1, D))


def _sigmoid(x):
    return 0.5 * jnp.tanh(0.5 * x) + 0.5


def _halves(arr):
    J = arr.shape[0] // 2
    view = arr.reshape(2, J, *arr.shape[1:])

    def spec(rows, row_index):
        return pl.BlockSpec((2, None, rows, arr.shape[-1]), lambda j, i: (0, j, row_index(i), 0))

    return view, spec


def ffn_up_gate(x, w, cw, cb, name, exch=None, tm=256):
    T, D = x.shape
    J2, FB, _ = w.shape
    J = J2 // 2
    tm = _row_tile(T, tm)

    def body(x_ref, u_ref, w_ref, b_ref, h_ref, c_ref, a_ref, halo):
        i = pl.program_id(1)

        @pl.when(i == 0)
        def _():
            halo[...] = jnp.zeros_like(halo)

        x_t = x_ref[...]
        conv = []
        for s in range(2):
            cur = lax.dot_general(x_t, u_ref[s], NT_DIMS, preferred_element_type=f32)
            h_ref[s] = cur
            cat = jnp.concatenate([halo[s], cur], axis=0)
            halo[s] = cur[tm - CONV_HALO:, :]
            out = b_ref[s] + w_ref[s, 0:1, :] * pltpu.roll(cat, 2, 0)[CONV_HALO:, :]
            out = out + w_ref[s, 1:2, :] * pltpu.roll(cat, 1, 0)[CONV_HALO:, :]
            conv.append(out + w_ref[s, 2:3, :] * cur)
            c_ref[s] = conv[s]
        a_ref[...] = (conv[0] * _sigmoid(conv[0]) * conv[1]).astype(bf16)

    (u4, u_spec), (w4, w_spec), (b4, b_spec) = _halves(w), _halves(cw), _halves(cb)
    hc_spec = pl.BlockSpec((2, None, tm, FB), lambda j, i: (0, j, i, 0))
    hc_shape = jax.ShapeDtypeStruct((2, J, T, FB), f32)
    res = _run(
        body, (x, u4, w4, b4), exch, name=name, grid=(J, T // tm),
        in_specs=[pl.BlockSpec((tm, D), lambda j, i: (i, 0)), u_spec(FB, lambda i: 0), w_spec(3, lambda i: 0), b_spec(1, lambda i: 0)],
        out_specs=[hc_spec, hc_spec, pl.BlockSpec((None, tm, FB), lambda j, i: (j, i, 0))],
        out_shape=[hc_shape, hc_shape, jax.ShapeDtypeStruct((J, T, FB), bf16)],
        scratch_shapes=[pltpu.VMEM((2, CONV_HALO, FB), f32)], sem=("parallel", "arbitrary"))
    (h, c, a), got = res if exch is not None else (res, None)
    out = (h.reshape(J2, T, FB), c.reshape(J2, T, FB), a)
    return out if exch is None else (out, got)


def ffn_down_gate_bwd(ds, wd, h, c, cw, name, tm=256):
    J2, T, FB = h.shape
    J = J2 // 2
    D = ds.shape[1]
    tm = _row_tile(T, tm)
    nt = T // tm
    n = tm + CONV_HALO

    def body(ds_ref, wd_ref, h_ref, c_ref, w_ref, dh_ref, dw_ref, db_ref, halo):
        i = pl.program_id(1)

        @pl.when(i == 0)
        def _():
            for r in (halo, dw_ref, db_ref):
                r[...] = jnp.zeros_like(r)

        da_t = lax.dot_general(ds_ref[...], wd_ref[...], NT_DIMS, preferred_element_type=f32)
        gate, val = c_ref[0], c_ref[1]
        sg = _sigmoid(gate)
        ds = (da_t * val * (sg * (1.0 + gate * (1.0 - sg))), da_t * (gate * sg))
        for s in range(2):
            d, cur = ds[s], h_ref[s]
            cat = jnp.concatenate([d, halo[s]], axis=0)
            halo[s] = d[:CONV_HALO, :]
            d1 = pltpu.roll(cat, n - 1, 0)[:tm, :]
            d2 = pltpu.roll(cat, n - 2, 0)[:tm, :]
            db_ref[s] += jnp.sum(d, axis=0, keepdims=True)
            dw_ref[s, 0:1, :] += jnp.sum(d2 * cur, axis=0, keepdims=True)
            dw_ref[s, 1:2, :] += jnp.sum(d1 * cur, axis=0, keepdims=True)
            dw_ref[s, 2:3, :] += jnp.sum(d * cur, axis=0, keepdims=True)
            dh_ref[s] = (w_ref[s, 2:3, :] * d + w_ref[s, 1:2, :] * d1 + w_ref[s, 0:1, :] * d2).astype(bf16)

    (h4, h_spec), (c4, _), (w4, w_spec) = _halves(h), _halves(c), _halves(cw)
    rev = lambda i: nt - 1 - i
    b_spec = pl.BlockSpec((2, None, 1, FB), lambda j, i: (0, j, 0, 0))
    dh, dw, db = _pcall(
        body, name=name, grid=(J, nt),
        in_specs=[pl.BlockSpec((tm, D), lambda j, i: (nt - 1 - i, 0)), pl.BlockSpec((None, FB, D), lambda j, i: (j, 0, 0)),
                  h_spec(tm, rev), h_spec(tm, rev), w_spec(3, lambda i: 0)],
        out_specs=[h_spec(tm, rev), w_spec(3, lambda i: 0), b_spec],
        out_shape=[jax.ShapeDtypeStruct((2, J, T, FB), bf16), jax.ShapeDtypeStruct((2, J, 3, FB), f32),
                   jax.ShapeDtypeStruct((2, J, 1, FB), f32)],
        scratch_shapes=[pltpu.VMEM((2, CONV_HALO, FB), f32)], compiler_params=_cp("parallel", "arbitrary"))(ds, wd, h4, c4, w4)
    return dh.reshape(J2, T, FB), dw.reshape(J2, 3, FB), db.reshape(J2, 1, FB)


def _sum_rhs(tk, strict):
    r = lax.broadcasted_iota(jnp.int32, (2 * tk, 2 * tk), 0) % tk
    c = lax.broadcasted_iota(jnp.int32, (2 * tk, 2 * tk), 1)
    return jnp.logical_or(c >= tk, (r > c) if strict else (r >= c)).astype(bf16)


def _split_sums(x, rhs):
    hi = x.astype(bf16)
    lo = (x - hi.astype(f32)).astype(bf16)
    return jnp.dot(jnp.concatenate([hi, lo], axis=1), rhs, preferred_element_type=f32)


def _key_offset(tq, tk):
    return lax.broadcasted_iota(jnp.int32, (tq, tk), 1) - lax.broadcasted_iota(jnp.int32, (tq, tk), 0)


def _attn_geometry(T, tq, tk, nsub):
    tq = _row_tile(T, tq)
    nsub = max(1, min(nsub, T // tq))
    while T % (nsub * tq):
        nsub -= 1
    return tq, min(tk, tq), nsub


def _chain_block(it, nkb, tk):
    return it < nkb, pl.multiple_of(jnp.maximum(nkb - 1 - it, 0) * tk, tk)


def _sweep_live(it, rs, nkb):
    m = None
    for r, n in zip(rs, nkb):
        ra = jnp.where(it < n, r, 2.0 * EXP_UNDERFLOW)
        m = ra if m is None else jnp.maximum(m, ra)
    return jnp.max(m) > EXP_UNDERFLOW


def attn_fwd(qkv, name, exch=None, tq=256, tk=128, nsub=4):
    T, D3 = qkv.shape
    D = D3 // 3
    H = D // HEAD_DIM
    tq, tk, nsub = _attn_geometry(T, tq, tk, nsub)
    bq = nsub * tq
    scale = HEAD_DIM ** -0.5

    never = -(tq + tk)
    R = range(nsub)

    def body(q_ref, k_ref, v_ref, o_ref, of_ref):
        i = pl.program_id(1)
        rhs = _sum_rhs(tk, False)
        cmr = _key_offset(tq, tk)
        nkb = [(i * nsub + a + 1) * (tq // tk) for a in R]
        qs = [q_ref[a * tq:(a + 1) * tq, :] for a in R]

        def cond(c):
            return _sweep_live(c[0], c[1], nkb)

        def step(c):
            it, rs, accs, fines = c
            vk = [_chain_block(it, nkb[a], tk) for a in R]
            k0 = [vk[a][1] for a in R]
            off = (it + 1) * tk - tq
            zs = [lax.dot_general(qs[a], k_ref[pl.ds(k0[a], tk), :], NT_DIMS, preferred_element_type=f32) * scale for a in R]
            masks = [cmr < jnp.where(vk[a][0], off, never) for a in R]
            lsms = [jnp.where(masks[a], jnp.minimum(-zs[a], 0.0) - jnp.log(1.0 + jnp.exp(-jnp.abs(zs[a]))), 0.0) for a in R]
            lsum = [_split_sums(lsms[a], rhs) for a in R]
            ws = [jnp.where(masks[a], jnp.exp(zs[a] + lsum[a][:, :tk] + rs[a]), 0.0) for a in R]
            his = [ws[a].astype(bf16) for a in R]
            los = [(ws[a] - his[a].astype(f32)).astype(bf16) for a in R]
            accs2 = [accs[a] + jnp.dot(his[a], v_ref[pl.ds(k0[a], tk), :], preferred_element_type=f32) for a in R]
            fines2 = [fines[a] + jnp.dot(los[a], v_ref[pl.ds(k0[a], tk), :], preferred_element_type=f32) for a in R]
            return it + 1, [rs[a] + lsum[a][:, tk:] for a in R], accs2, fines2

        zeros = [jnp.zeros((tq, HEAD_DIM), f32)] * nsub
        _, _, accs, fines = lax.while_loop(cond, step, (jnp.int32(0), [jnp.zeros((tq, tk), f32)] * nsub, zeros, zeros))
        for a in R:
            o_ref[a * tq:(a + 1) * tq, :] = accs[a].astype(bf16)
            of_ref[a * tq:(a + 1) * tq, :] = accs[a] + fines[a]

    qblk = pl.BlockSpec((bq, HEAD_DIM), lambda h, i: (i, h))
    return _run(
        body, (qkv, qkv, qkv), exch, name=name, grid=(H, T // bq),
        in_specs=[qblk, pl.BlockSpec((T, HEAD_DIM), lambda h, i: (0, H + h)),
                  pl.BlockSpec((T, HEAD_DIM), lambda h, i: (0, 2 * H + h))],
        out_specs=[qblk, qblk], out_shape=[jax.ShapeDtypeStruct((T, D), bf16), jax.ShapeDtypeStruct((T, D), f32)],
        sem=("parallel", "arbitrary"))


def attn_bwd(qkv, do, o_fine, name, exch=None, tq=256, tk=128, nsub=4):
    T, D3 = qkv.shape
    D = D3 // 3
    H = D // HEAD_DIM
    tq, tk, nsub = _attn_geometry(T, tq, tk, nsub)
    assert tk == HEAD_DIM
    bq = nsub * tq
    nq = T // bq
    scale = HEAD_DIM ** -0.5
    never = -(tq + tk)
    R = range(nsub)

    def body(q_ref, k_ref, v_ref, do_ref, of_ref, dq_ref, dk_ref, dv_ref, dk_acc, dv_acc):
        i = pl.program_id(1)

        @pl.when(i == 0)
        def _():
            dk_acc[...] = jnp.zeros_like(dk_acc)
            dv_acc[...] = jnp.zeros_like(dv_acc)

        rhs_incl, rhs_excl = _sum_rhs(tk, False), _sum_rhs(tk, True)
        cmr = _key_offset(tq, tk)
        nkb = [(i * nsub + a + 1) * (tq // tk) for a in R]
        qs = [q_ref[a * tq:(a + 1) * tq, :] for a in R]
        dos = [do_ref[a * tq:(a + 1) * tq, :] for a in R]
        zeros = [jnp.zeros((tq, tk), f32)] * nsub

        def cond(c):
            return _sweep_live(c[0], c[1], nkb)

        def weights(it, rs):
            vk = [_chain_block(it, nkb[a], tk) for a in R]
            k0 = [vk[a][1] for a in R]
            off = (it + 1) * tk - tq
            zs = [lax.dot_general(qs[a], k_ref[pl.ds(k0[a], tk), :], NT_DIMS, preferred_element_type=f32) * scale for a in R]
            das = [lax.dot_general(dos[a], v_ref[pl.ds(k0[a], tk), :], NT_DIMS, preferred_element_type=f32) for a in R]
            masks = [cmr < jnp.where(vk[a][0], off, never) for a in R]
            es = [jnp.exp(-jnp.abs(zs[a])) for a in R]
            lsms = [jnp.where(masks[a], jnp.minimum(-zs[a], 0.0) - jnp.log(1.0 + es[a]), 0.0) for a in R]
            lsum = [_split_sums(lsms[a], rhs_incl) for a in R]
            ws = [jnp.where(masks[a], jnp.exp(zs[a] + lsum[a][:, :tk] + rs[a]), 0.0) for a in R]
            gs = [ws[a] * das[a] for a in R]
            gsum = [_split_sums(gs[a], rhs_excl) for a in R]
            return k0, masks, zs, es, ws, gs, lsum, gsum

        g_total = [_split_sums(dos[a].astype(f32) * of_ref[a * tq:(a + 1) * tq, :], rhs_incl)[:, tk:] for a in R]

        def grad_step(c):
            it, rs, runs, dqs = c
            k0, masks, zs, es, ws, gs, lsum, gsum = weights(it, rs)
            sig = [_sigmoid(zs[a]) for a in R]
            prefix = [g_total[a] - (runs[a] + gsum[a][:, :tk]) for a in R]
            dzs = [(jnp.where(masks[a], gs[a] - sig[a] * prefix[a], 0.0) * scale).astype(bf16) for a in R]
            wbs = [ws[a].astype(bf16) for a in R]
            dqs2 = [dqs[a] + jnp.dot(dzs[a], k_ref[pl.ds(k0[a], tk), :], preferred_element_type=f32) for a in R]
            dks = [lax.dot_general(dzs[a], qs[a], TN_DIMS, preferred_element_type=f32) for a in R]
            dvs = [lax.dot_general(wbs[a], dos[a], TN_DIMS, preferred_element_type=f32) for a in R]
            for a in R:
                dk_acc[pl.ds(k0[a], tk), :] += dks[a]
                dv_acc[pl.ds(k0[a], tk), :] += dvs[a]
            return it + 1, [rs[a] + lsum[a][:, tk:] for a in R], [runs[a] + gsum[a][:, tk:] for a in R], dqs2

        init = (jnp.int32(0), zeros, zeros, [jnp.zeros((tq, HEAD_DIM), f32)] * nsub)
        _, _, _, dqs = lax.while_loop(cond, grad_step, init)
        for a in R:
            dq_ref[a * tq:(a + 1) * tq, :] = dqs[a].astype(bf16)

        @pl.when(i == nq - 1)
        def _():
            dk_ref[...] = dk_acc[...].astype(bf16)
            dv_ref[...] = dv_acc[...].astype(bf16)

    qblk = pl.BlockSpec((bq, HEAD_DIM), lambda h, i: (i, h))
    head = pl.BlockSpec((T, HEAD_DIM), lambda h, i: (0, h))
    return _run(
        body, (qkv, qkv, qkv, do, o_fine), exch, name=name, grid=(H, nq),
        in_specs=[qblk, pl.BlockSpec((T, HEAD_DIM), lambda h, i: (0, H + h)),
                  pl.BlockSpec((T, HEAD_DIM), lambda h, i: (0, 2 * H + h)), qblk, qblk],
        out_specs=[qblk, head, head], out_shape=[jax.ShapeDtypeStruct((T, D), bf16)] * 3,
        scratch_shapes=[pltpu.VMEM((T, HEAD_DIM), f32), pltpu.VMEM((T, HEAD_DIM), f32)],
        sem=("parallel", "arbitrary"))


def sum_partials(p, name, tr=256):
    K, R, C = p.shape
    tr = _row_tile(R, tr)

    def body(p_ref, o_ref):
        g = p_ref[0].astype(f32)
        for k in range(1, K):
            g = g + p_ref[k].astype(f32)
        o_ref[...] = g

    return _pcall(
        body, name=name, grid=(R // tr,), in_specs=[pl.BlockSpec((K, tr, C), lambda i: (0, i, 0))],
        out_specs=pl.BlockSpec((tr, C), lambda i: (i, 0)), out_shape=jax.ShapeDtypeStruct((R, C), f32),
        compiler_params=_cp("parallel"))(p)


def adamw(p, w, m, v, slot, outs, name):
    K, R, C = p.shape
    L = w.shape[0]
    tc = _lane_tile(C, 1024)
    row_bytes = 2 * tc * (K * p.dtype.itemsize + 7 * 4)
    tr = _row_tile(R, max(8, MM_VMEM_BUDGET_BYTES // row_bytes // 8 * 8))
    c1 = 1.0 - ADAM_B1 ** ADAM_STEP
    c2 = 1.0 - ADAM_B2 ** ADAM_STEP
    n_old = 0 if outs is None else 4

    def body(p_ref, w_ref, m_ref, v_ref, *refs):
        g_ref, d_ref, nm_ref, nv_ref = refs[n_old:]
        g = p_ref[0].astype(f32)
        for k in range(1, K):
            g = g + p_ref[k].astype(f32)
        nm = ADAM_B1 * m_ref[...] + (1.0 - ADAM_B1) * g
        nv = ADAM_B2 * v_ref[...] + (1.0 - ADAM_B2) * (g * g)
        g_ref[...] = g
        nm_ref[...] = nm
        nv_ref[...] = nv
        d_ref[...] = -ADAM_LR * ((nm / c1) / (jnp.sqrt(nv / c2) + ADAM_EPS) + ADAM_WD * w_ref[...])

    blk = pl.BlockSpec((None, tr, tc), lambda i, j: (slot, i, j))
    old = [] if outs is None else list(outs)
    return _pcall(
        body, name=name, grid=(R // tr, C // tc),
        in_specs=[pl.BlockSpec((K, tr, tc), lambda i, j: (0, i, j)), blk, blk, blk] + [pl.BlockSpec(memory_space=pl.ANY)] * n_old,
        out_specs=[blk] * 4, out_shape=[jax.ShapeDtypeStruct((L, R, C), f32)] * 4,
        input_output_aliases={4 + t: t for t in range(n_old)}, compiler_params=_cp("parallel", "parallel"))(p, w, m, v, *old)


def _pack(parts):
    flat, slices, off = [], [], 0
    for a in parts:
        nel = a.size
        pad = -nel % PACK_ALIGN
        flat.append(jnp.pad(a.reshape(-1).astype(f32), (0, pad)))
        slices.append((off, nel, a.shape))
        off += nel + pad
    return jnp.concatenate(flat).reshape(-1, 128), slices


def _unpack(packed, slices):
    flat = packed.reshape(-1)
    return [flat[off:off + nel].reshape(shape) for off, nel, shape in slices]


def kernel(x, pool_w, pool_scale, attn_w_qkv, attn_w_o, ffn_w_up, ffn_conv_w, ffn_conv_b, ffn_w_down, ln_mix_g, ln_mix_b, ln_ffn_g, ln_ffn_b, loss_target, m_pool_w, m_pool_scale, m_attn_w_qkv, m_attn_w_o, m_ffn_w_up, m_ffn_conv_w, m_ffn_conv_b, m_ffn_w_down, m_ln_mix_g, m_ln_mix_b, m_ln_ffn_g, m_ln_ffn_b, v_pool_w, v_pool_scale, v_attn_w_qkv, v_attn_w_o, v_ffn_w_up, v_ffn_conv_w, v_ffn_conv_b, v_ffn_w_down, v_ln_mix_g, v_ln_mix_b, v_ln_ffn_g, v_ln_ffn_b):
    _, T, D = x.shape
    depth = ln_mix_g.shape[0]
    alpha = (2.0 * depth) ** 0.25
    G, CS, C = pool_w.shape[1:]
    FB = ffn_w_up.shape[2]
    JH = N_DEV // 2
    me = 4 * lax.axis_index("x") + 2 * lax.axis_index("y") + lax.axis_index("c")

    up_t, m_up_t, v_up_t = (jnp.swapaxes(a, 1, 2) for a in (ffn_w_up, m_ffn_w_up, v_ffn_w_up))
    shards = {}
    for l in range(depth):
        j = l // 2
        shards["up", l], shards["down", l] = up_t[l].astype(bf16), ffn_w_down[l].astype(bf16)
        if l % 2 == 0:
            shards["pool", l] = pool_w[j].astype(bf16)
        else:
            shards["qkv", l], shards["wo", l] = attn_w_qkv[j].astype(bf16), attn_w_o[j].astype(bf16)
    full = {}

    def gather_in(call, keys):
        keys = [k for k in keys if k[1] < depth]
        if not keys:
            return call(None)
        res, got = call(([shards[k] for k in keys], False))
        full.update(zip(keys, got))
        return res

    full["pool", 0], cw_all = all_gather([shards["pool", 0], ffn_conv_w], "gather_first")

    xf = x[0]
    xb = xf.astype(bf16)
    saved, weights = [], []
    for l in range(depth):
        j, sv, wl = l // 2, {}, {}
        if l % 2 == 0:
            wl["pool"] = full["pool", l].transpose(1, 0, 2, 3).reshape(G, C, C)
            u, sv["pooled"] = gather_in(lambda e: pool_fwd(xf, wl["pool"], pool_scale[j], f"pool_fwd_l{l}", exch=e),
                                        [("up", 0)] if l == 0 else [])
            up_rides = ([("down", l)] if l == 0 else [("qkv", l + 1)]) + [("wo", l + 1)]
            down_rides = [("qkv", l + 1)] if l == 0 else []
        else:
            wl["qkv"], wl["wo"] = full["qkv", l], full["wo", l].reshape(1, D, D)
            sv["xin_b"] = xb
            sv["qkv"] = mm_nn(xb, wl["qkv"], bf16, f"qkv_l{l}", out_cols=True)
            sv["o"], sv["o_fine"] = gather_in(lambda e: attn_fwd(sv["qkv"], f"attn_fwd_l{l}", exch=e),
                                [("up", l), ("down", l), ("pool", l + 1)])
            u = mm_nn(sv["o"], wl["wo"], f32, f"attn_out_l{l}")[0]
            up_rides, down_rides = [("up", l + 1)], [("down", l + 1)]
        x1, sv["x1_b"], sv["xhat1"], sv["rstd1"] = ln_fwd(xf, u, ln_mix_g[l], ln_mix_b[l], alpha, f"ln_mix_l{l}")
        wl["up"], wl["cw"], wl["cb"] = full["up", l], cw_all[:, l], ffn_conv_b[l].reshape(N_DEV, 1, FB)
        sv["h"], sv["c"], sv["a"] = gather_in(
            lambda e: ffn_up_gate(sv["x1_b"], wl["up"], wl["cw"], wl["cb"], f"ffn_up_l{l}", exch=e), up_rides)
        wl["down"] = full["down", l].reshape(JH, FB, D)
        f = gather_in(lambda e: mm_nn_sum(sv["a"], wl["down"], f"ffn_down_l{l}", exch=e), down_rides)
        xf, xb, sv["xhat2"], sv["rstd2"] = ln_fwd(x1, f, ln_ffn_g[l], ln_ffn_b[l], alpha, f"ln_ffn_l{l}")
        saved.append(sv)
        weights.append(wl)

    dx, loss_row = loss_head(xf, loss_target[0], "loss_head")

    landed = {}

    def scatter_in(call, parts):
        res, got = call(([p for _, p in parts], True))
        landed.update(zip([k for k, _ in parts], got))
        return res

    grads = [None] * depth
    res, pending = None, []
    for l in reversed(range(depth)):
        wl, sv, j, gl = weights[l], saved[l], l // 2, {}
        ds2, ds2_b, gl["ln_ffn_g"], gl["ln_ffn_b"] = ln_bwd(dx, res, alpha, sv["xhat2"], sv["rstd2"], ln_ffn_g[l], f"ln_ffn_bwd_l{l}")
        g_down = mm_tn(sv["a"], ds2_b[None], bf16, f"ffn_down_dw_l{l}").reshape(N_DEV, FB // 2, D)
        dh, gl["cw"], gl["cb"] = ffn_down_gate_bwd(ds2_b, wl["down"], sv["h"], sv["c"], wl["cw"], f"ffn_down_dx_l{l}")
        g_up = scatter_in(lambda e: mm_tn(dh, sv["x1_b"][None], bf16, f"ffn_up_dw_l{l}", exch=e),
                          [(("down", l), g_down)] + pending)
        dx1 = scatter_in(lambda e: mm_nn_sum(dh, wl["up"], f"ffn_up_dx_l{l}", exch=e), [(("up", l), g_up)])
        ds1, ds1_b, gl["ln_mix_g"], gl["ln_mix_b"] = ln_bwd(dx1, ds2, alpha, sv["xhat1"], sv["rstd1"], ln_mix_g[l], f"ln_mix_bwd_l{l}")
        if l % 2 == 0:
            dx, g_pool, gl["pool_scale"] = pool_bwd(ds1, sv["pooled"], wl["pool"], pool_scale[j], alpha, f"pool_bwd_l{l}")
            res, pending = None, [(("pool", l), g_pool.reshape(G, N_DEV, CS, C).transpose(1, 0, 2, 3).astype(bf16))]
        else:
            do = mm_nt(ds1_b, wl["wo"], bf16, f"attn_out_dx_l{l}")[0]
            g_wo = mm_tn(sv["o"][None], ds1_b[None], bf16, f"attn_out_dw_l{l}").reshape(N_DEV, D // N_DEV, D)
            dqkv = jnp.concatenate(scatter_in(lambda e: attn_bwd(sv["qkv"], do, sv["o_fine"], f"attn_bwd_l{l}", exch=e),
                                              [(("wo", l), g_wo)]), axis=1)
            g_qkv = mm_tn(sv["xin_b"][None], dqkv, bf16, f"qkv_dw_l{l}", b_cols=N_DEV)
            dx = scatter_in(lambda e: mm_nt_sum(dqkv, wl["qkv"], f"qkv_dx_l{l}", a_cols=True, exch=e), [(("qkv", l), g_qkv)])
            res, pending = ds1, []
        grads[l] = gl
    assert res is None, "the first layer is a pooling layer: its backward adds the residual path itself"
    grad_x = dx[None]
    if pending:
        landed.update(zip([k for k, _ in pending], scatter_partials([p for _, p in pending], "scatter_last")))

    def stack_update(name, kind, w, m, v):
        shape = w.shape
        rows = w[0].size // shape[-1]
        flat = [a.reshape(shape[0], rows, shape[-1]) for a in (w, m, v)]
        outs = None
        for i, l in enumerate(sorted(l for k, l in landed if k == kind)):
            outs = adamw(landed[kind, l].reshape(N_DEV, rows, shape[-1]), *flat, i, outs, f"adamw_{name}_{i}")
        return [o.reshape(shape) for o in outs]

    big = {
        "pool_w": stack_update("pool_w", "pool", pool_w, m_pool_w, v_pool_w),
        "attn_w_qkv": stack_update("attn_w_qkv", "qkv", attn_w_qkv, m_attn_w_qkv, v_attn_w_qkv),
        "attn_w_o": stack_update("attn_w_o", "wo", attn_w_o, m_attn_w_o, v_attn_w_o),
        "ffn_w_up": [jnp.swapaxes(o, 1, 2) for o in stack_update("ffn_w_up", "up", up_t, m_up_t, v_up_t)],
        "ffn_w_down": stack_update("ffn_w_down", "down", ffn_w_down, m_ffn_w_down, v_ffn_w_down),
    }

    def per_layer(key):
        return jnp.stack([grads[l][key].reshape(-1) for l in range(depth)])

    small_parts = [per_layer("ln_mix_g"), per_layer("ln_mix_b"), per_layer("ln_ffn_g"), per_layer("ln_ffn_b"),
                   jnp.stack([grads[l]["pool_scale"].reshape(-1) for l in range(0, depth, 2)]),
                   per_layer("cb"), jnp.stack([grads[l]["cw"] for l in range(depth)]), loss_row[0, :1]]
    packed, slices = _pack(small_parts)
    total = sum_partials(all_gather([packed], "gather_small")[0], "sum_small")
    g_mix_g, g_mix_b, g_ffn_g, g_ffn_b, g_scale, g_cb, g_cw_all, loss = _unpack(total, slices)
    g_cw = lax.dynamic_index_in_dim(g_cw_all, me, axis=1, keepdims=False)

    small_w = [ln_mix_g, ln_mix_b, ln_ffn_g, ln_ffn_b, pool_scale, ffn_conv_b, ffn_conv_w]
    small_m = [m_ln_mix_g, m_ln_mix_b, m_ln_ffn_g, m_ln_ffn_b, m_pool_scale, m_ffn_conv_b, m_ffn_conv_w]
    small_v = [v_ln_mix_g, v_ln_mix_b, v_ln_ffn_g, v_ln_ffn_b, v_pool_scale, v_ffn_conv_b, v_ffn_conv_w]
    small_g = [g_mix_g, g_mix_b, g_ffn_g, g_ffn_b, g_scale, g_cb.reshape(ffn_conv_b.shape), g_cw]
    pw, wslices = _pack(small_w)
    res = adamw(_pack(small_g)[0][None], pw[None], _pack(small_m)[0][None], _pack(small_v)[0][None], 0, None, "adamw_small")
    small = dict(zip(["ln_mix_g", "ln_mix_b", "ln_ffn_g", "ln_ffn_b", "pool_scale", "ffn_conv_b", "ffn_conv_w"],
                     zip(*[_unpack(r[0], wslices) for r in res])))

    order = ["pool_w", "pool_scale", "attn_w_qkv", "attn_w_o", "ffn_w_up", "ffn_conv_w", "ffn_conv_b", "ffn_w_down",
             "ln_mix_g", "ln_mix_b", "ln_ffn_g", "ln_ffn_b"]
    table = {**big, **{k: list(val) for k, val in small.items()}}
    outs = [loss.reshape(()), grad_x]
    for t in range(4):
        outs += [table[name][t] for name in order]
    return tuple(outs)
```

```python
import jax
import jax.numpy as jnp
from jax import lax
from jax.experimental import pallas as pl
from jax.experimental.pallas import tpu as pltpu

f32, bf16 = jnp.float32, jnp.bfloat16

N_DEV = 8
HEAD_DIM = 128
POOL_WINDOWS = (2, 4, 8, 16)
POOL_HALO = 16
CONV_HALO = 8
LN_EPS = 1e-5
ADAM_LR, ADAM_B1, ADAM_B2, ADAM_EPS, ADAM_WD, ADAM_STEP = 0.001, 0.9, 0.999, 1e-08, 0.01, 10
V7X_VMEM_BYTES = 64 * 1024 * 1024
VMEM_LIMIT_BYTES = V7X_VMEM_BYTES - 8 * 1024 * 1024
MM_VMEM_BUDGET_BYTES = 40 * 1024 * 1024
EXP_UNDERFLOW = -104.0
PACK_ALIGN = 8 * 128

NT_DIMS = (((1,), (1,)), ((), ()))
TN_DIMS = (((0,), (0,)), ((), ()))


def _pcall(body, **kw):
    return pl.pallas_call(body, **kw)


def _cp(*sem):
    return pltpu.CompilerParams(dimension_semantics=sem, vmem_limit_bytes=VMEM_LIMIT_BYTES)


def _row_tile(rows, cap):
    if rows <= cap:
        return rows
    best = None
    for t in range(8, cap + 1, 8):
        if rows % t == 0:
            best = t
    assert best is not None, rows
    return best


def _lane_tile(cols, cap):
    best = cols
    for t in range(128, min(cap, cols) + 1, 128):
        if cols % t == 0:
            best = t
    return best if cols > cap else cols


N_PEER = N_DEV - 1


def _mesh_place():
    x, y, c = lax.axis_index("x"), lax.axis_index("y"), lax.axis_index("c")
    peers = []
    for k in range(1, N_DEV):
        px = 1 - x if k & 4 else x
        py = 1 - y if k & 2 else y
        pc = 1 - c if k & 1 else c
        peers.append(((px, py, pc), 4 * px + 2 * py + pc))
    return 4 * x + 2 * y + c, peers


def _exchange_copies(ins, outs, send, recv, loc, place, scatter):
    me, peers = place

    def src(a, idx):
        return ins[a].at[idx] if scatter else ins[a]

    local, starts, waits = [], [], []
    for a in range(len(ins)):
        local.append(pltpu.make_async_copy(src(a, me), outs[a].at[me], loc.at[a]))
        for k, (dev, idx) in enumerate(peers):
            sems = dict(send_sem=send.at[a * N_PEER + k], recv_sem=recv.at[a * N_PEER + k],
                        device_id=dev, device_id_type=pl.DeviceIdType.MESH)
            starts.append(pltpu.make_async_remote_copy(src_ref=src(a, idx), dst_ref=outs[a].at[me], **sems))
            waits.append(pltpu.make_async_remote_copy(src_ref=src(a, idx), dst_ref=outs[a].at[idx], **sems))
    return local, starts, waits


def _exchange_shapes(arrs, scatter):
    return [jax.ShapeDtypeStruct(a.shape if scatter else (N_DEV, *a.shape), a.dtype) for a in arrs]


def _exchange_sems(n):
    return [pltpu.SemaphoreType.DMA((n * N_PEER,)), pltpu.SemaphoreType.DMA((n * N_PEER,)), pltpu.SemaphoreType.DMA((n,))]


def _exchange(arrs, scatter, name):
    n = len(arrs)

    def body(*refs):
        local, starts, waits = _exchange_copies(refs[:n], refs[n:2 * n], *refs[2 * n:], _mesh_place(), scatter)
        for cp in local + starts:
            cp.start()
        for cp in waits + local:
            cp.wait()

    any_spec = pl.BlockSpec(memory_space=pl.ANY)
    return _pcall(body, name=name, in_specs=[any_spec] * n, out_specs=[any_spec] * n,
                  out_shape=_exchange_shapes(arrs, scatter), scratch_shapes=_exchange_sems(n))(*arrs)


def all_gather(shards, name):
    return _exchange(shards, False, name)


def scatter_partials(partials, name):
    return _exchange(partials, True, name)


def _run(body, args, exch, *, name, grid, in_specs, out_specs, out_shape, scratch_shapes=(), sem):
    scratch_shapes = list(scratch_shapes)
    if exch is None:
        return _pcall(body, name=name, grid=grid, in_specs=in_specs, out_specs=out_specs, out_shape=out_shape,
                      scratch_shapes=scratch_shapes, compiler_params=_cp(*sem))(*args)
    arrs, scatter = exch
    single = not isinstance(out_shape, (list, tuple))
    out_specs, out_shape = ([out_specs], [out_shape]) if single else (list(out_specs), list(out_shape))
    n, na, nb, ns = len(arrs), len(in_specs), len(out_specs), len(scratch_shapes)

    def hosted(*refs):
        ins, xin = refs[:na], refs[na:na + n]
        outs, xout = refs[na + n:na + n + nb], refs[na + n + nb:na + 2 * n + nb]
        scr, sems = refs[na + 2 * n + nb:na + 2 * n + nb + ns], refs[na + 2 * n + nb + ns:]
        first, last = None, None
        for d, size in enumerate(grid):
            pid = pl.program_id(d)
            first = (pid == 0) if first is None else jnp.logical_and(first, pid == 0)
            last = (pid == size - 1) if last is None else jnp.logical_and(last, pid == size - 1)
        local, starts, waits = _exchange_copies(xin, xout, *sems, _mesh_place(), scatter)

        @pl.when(first)
        def _():
            for cp in local + starts:
                cp.start()

        body(*ins, *outs, *scr)

        @pl.when(last)
        def _():
            for cp in waits + local:
                cp.wait()

    any_spec = pl.BlockSpec(memory_space=pl.ANY)
    res = _pcall(hosted, name=name, grid=grid, in_specs=list(in_specs) + [any_spec] * n,
                 out_specs=out_specs + [any_spec] * n, out_shape=out_shape + _exchange_shapes(arrs, scatter),
                 scratch_shapes=scratch_shapes + _exchange_sems(n),
                 compiler_params=_cp(*(["arbitrary"] * len(grid))))(*args, *arrs)
    return (res[0] if single else list(res[:nb])), list(res[nb:])


def mm_nn(a, b, out_dtype, name, out_cols=False, exch=None, tm=512):
    M, K = a.shape
    J, _, N = b.shape
    tm = _row_tile(M, tm)

    def body(a_ref, b_ref, o_ref):
        o_ref[...] = jnp.dot(a_ref[...], b_ref[...], preferred_element_type=f32).astype(o_ref.dtype)

    if out_cols:
        out_spec, out_shape = pl.BlockSpec((tm, N), lambda j, i: (i, j)), (M, J * N)
    else:
        out_spec, out_shape = pl.BlockSpec((None, tm, N), lambda j, i: (j, i, 0)), (J, M, N)
    return _run(
        body, (a, b), exch, name=name, grid=(J, M // tm),
        in_specs=[pl.BlockSpec((tm, K), lambda j, i: (i, 0)), pl.BlockSpec((None, K, N), lambda j, i: (j, 0, 0))],
        out_specs=out_spec, out_shape=jax.ShapeDtypeStruct(out_shape, out_dtype), sem=("parallel", "parallel"))


def mm_nn_sum(a, b, name, exch=None, tm=1024):
    J, M, K = a.shape
    N = b.shape[2]
    tm = _row_tile(M, tm)

    def body(a_ref, b_ref, o_ref):
        j = pl.program_id(1)
        p = jnp.dot(a_ref[...], b_ref[...], preferred_element_type=f32)

        @pl.when(j == 0)
        def _():
            o_ref[...] = p

        @pl.when(j > 0)
        def _():
            o_ref[...] += p

    return _run(
        body, (a, b), exch, name=name, grid=(M // tm, J),
        in_specs=[pl.BlockSpec((None, tm, K), lambda i, j: (j, i, 0)), pl.BlockSpec((None, K, N), lambda i, j: (j, 0, 0))],
        out_specs=pl.BlockSpec((tm, N), lambda i, j: (i, 0)), out_shape=jax.ShapeDtypeStruct((M, N), f32),
        sem=("parallel", "arbitrary"))


def mm_nt(a, b, out_dtype, name, exch=None, tm=512):
    M, K = a.shape
    J, N, _ = b.shape
    tm = _row_tile(M, tm)

    def body(a_ref, b_ref, o_ref):
        o_ref[...] = lax.dot_general(a_ref[...], b_ref[...], NT_DIMS, preferred_element_type=f32).astype(o_ref.dtype)

    return _run(
        body, (a, b), exch, name=name, grid=(J, M // tm),
        in_specs=[pl.BlockSpec((tm, K), lambda j, i: (i, 0)), pl.BlockSpec((None, N, K), lambda j, i: (j, 0, 0))],
        out_specs=pl.BlockSpec((None, tm, N), lambda j, i: (j, i, 0)), out_shape=jax.ShapeDtypeStruct((J, M, N), out_dtype),
        sem=("parallel", "parallel"))


def mm_nt_sum(a, b, name, a_cols=False, exch=None, tm=1024):
    J, N, K = b.shape
    M = a.shape[0] if a_cols else a.shape[1]
    tm = _row_tile(M, tm)

    def body(a_ref, b_ref, o_ref):
        j = pl.program_id(1)
        p = lax.dot_general(a_ref[...], b_ref[...], NT_DIMS, preferred_element_type=f32)

        @pl.when(j == 0)
        def _():
            o_ref[...] = p

        @pl.when(j > 0)
        def _():
            o_ref[...] += p

    a_spec = pl.BlockSpec((tm, K), lambda i, j: (i, j)) if a_cols else pl.BlockSpec((None, tm, K), lambda i, j: (j, i, 0))
    return _run(
        body, (a, b), exch, name=name, grid=(M // tm, J),
        in_specs=[a_spec, pl.BlockSpec((None, N, K), lambda i, j: (j, 0, 0))],
        out_specs=pl.BlockSpec((tm, N), lambda i, j: (i, 0)), out_shape=jax.ShapeDtypeStruct((M, N), f32),
        sem=("parallel", "arbitrary"))


def mm_tn(a, b, out_dtype, name, b_cols=0, exch=None, tm=1024, tn=1024, tk=2048):
    JA, T, M = a.shape
    if b_cols:
        JB, N = b_cols, b.shape[1] // b_cols
    else:
        JB, _, N = b.shape
    J = max(JA, JB)
    tm = _lane_tile(M, tm)
    tn = _lane_tile(N, tn)
    tk = _row_tile(T, tk)

    def vmem_bytes(tk):
        return 2 * 2 * tk * (tm + tn) + 2 * 4 * tm * tn + 2 * jnp.dtype(out_dtype).itemsize * tm * tn

    while vmem_bytes(tk) > MM_VMEM_BUDGET_BYTES and tk % 16 == 0:
        tk //= 2
    nk, nn = T // tk, N // tn

    def body(a_ref, b_ref, o_ref, acc):
        k = pl.program_id(3)
        p = lax.dot_general(a_ref[...], b_ref[...], TN_DIMS, preferred_element_type=f32)

        @pl.when(k == 0)
        def _():
            acc[...] = p

        @pl.when(k > 0)
        def _():
            acc[...] += p

        @pl.when(k == nk - 1)
        def _():
            o_ref[...] = acc[...].astype(o_ref.dtype)

    a_spec = pl.BlockSpec((None, tk, tm), (lambda j, i, n, k: (j, k, i)) if JA > 1 else (lambda j, i, n, k: (0, k, i)))
    if b_cols:
        b_spec = pl.BlockSpec((tk, tn), lambda j, i, n, k: (k, j * nn + n))
    else:
        b_spec = pl.BlockSpec((None, tk, tn), (lambda j, i, n, k: (j, k, n)) if JB > 1 else (lambda j, i, n, k: (0, k, n)))
    return _run(
        body, (a, b), exch, name=name, grid=(J, M // tm, nn, nk), in_specs=[a_spec, b_spec],
        out_specs=pl.BlockSpec((None, tm, tn), lambda j, i, n, k: (j, i, n)),
        out_shape=jax.ShapeDtypeStruct((J, M, N), out_dtype), scratch_shapes=[pltpu.VMEM((tm, tn), f32)],
        sem=("parallel", "parallel", "parallel", "arbitrary"))


def ln_fwd(x, u, g, b, alpha, name, tm=256):
    T, D = x.shape
    tm = _row_tile(T, tm)

    def body(x_ref, u_ref, g_ref, b_ref, y_ref, yb_ref, xh_ref, rs_ref):
        s = alpha * x_ref[...] + u_ref[...]
        mu = jnp.mean(s, axis=-1, keepdims=True)
        c = s - mu
        var = jnp.mean(c * c, axis=-1, keepdims=True)
        r = lax.rsqrt(var + LN_EPS)
        xh = c * r
        y = xh * g_ref[...] + b_ref[...]
        y_ref[...] = y
        yb_ref[...] = y.astype(bf16)
        xh_ref[...] = xh
        rs_ref[...] = r

    row = pl.BlockSpec((tm, D), lambda i: (i, 0))
    vec = pl.BlockSpec((1, D), lambda i: (0, 0))
    return _pcall(
        body, name=name, grid=(T // tm,), in_specs=[row, row, vec, vec],
        out_specs=[row, row, row, pl.BlockSpec((tm, 1), lambda i: (i, 0))],
        out_shape=[jax.ShapeDtypeStruct((T, D), f32), jax.ShapeDtypeStruct((T, D), bf16),
                   jax.ShapeDtypeStruct((T, D), f32), jax.ShapeDtypeStruct((T, 1), f32)],
        compiler_params=_cp("parallel"))(x, u, g.reshape(1, D), b.reshape(1, D))


def ln_bwd(dy, res, alpha, xhat, rstd, g, name, tm=256):
    T, D = dy.shape
    tm = _row_tile(T, tm)
    n_dy = 1 if res is None else 2

    def body(*refs):
        dy_refs, (xh_ref, rs_ref, g_ref, ds_ref, dsb_ref, dg_ref, db_ref) = refs[:n_dy], refs[n_dy:]
        i = pl.program_id(0)
        dy_t, xh = dy_refs[0][...], xh_ref[...]
        if res is not None:
            dy_t = dy_t + alpha * dy_refs[1][...]
        dxh = dy_t * g_ref[...]
        m1 = jnp.mean(dxh, axis=-1, keepdims=True)
        m2 = jnp.mean(dxh * xh, axis=-1, keepdims=True)
        ds = rs_ref[...] * (dxh - m1 - xh * m2)
        ds_ref[...] = ds
        dsb_ref[...] = ds.astype(bf16)
        pg = jnp.sum(dy_t * xh, axis=0, keepdims=True)
        pb = jnp.sum(dy_t, axis=0, keepdims=True)

        @pl.when(i == 0)
        def _():
            dg_ref[...] = pg
            db_ref[...] = pb

        @pl.when(i > 0)
        def _():
            dg_ref[...] += pg
            db_ref[...] += pb

    row = pl.BlockSpec((tm, D), lambda i: (i, 0))
    vec = pl.BlockSpec((1, D), lambda i: (0, 0))
    dys = (dy,) if res is None else (dy, res)
    return _pcall(
        body, name=name, grid=(T // tm,), in_specs=[row] * n_dy + [row, pl.BlockSpec((tm, 1), lambda i: (i, 0)), vec],
        out_specs=[row, row, vec, vec],
        out_shape=[jax.ShapeDtypeStruct((T, D), f32), jax.ShapeDtypeStruct((T, D), bf16),
                   jax.ShapeDtypeStruct((1, D), f32), jax.ShapeDtypeStruct((1, D), f32)],
        compiler_params=_cp("arbitrary"))(*dys, xhat, rstd, g.reshape(1, D))


def loss_head(y, target, name, tm=256):
    T, D = y.shape
    tm = _row_tile(T, tm)
    nt = T // tm

    def body(y_ref, t_ref, dy_ref, l_ref, acc):
        i = pl.program_id(0)
        e = y_ref[...] - t_ref[...]
        dy_ref[...] = e / D
        p = jnp.sum(e * e, axis=0, keepdims=True)

        @pl.when(i == 0)
        def _():
            acc[...] = p

        @pl.when(i > 0)
        def _():
            acc[...] += p

        @pl.when(i == nt - 1)
        def _():
            l_ref[...] = jnp.full(l_ref.shape, 0.5 * jnp.sum(acc[...]) / D, f32)

    row = pl.BlockSpec((tm, D), lambda i: (i, 0))
    return _pcall(
        body, name=name, grid=(nt,), in_specs=[row, row],
        out_specs=[row, pl.BlockSpec((1, 128), lambda i: (0, 0))],
        out_shape=[jax.ShapeDtypeStruct((T, D), f32), jax.ShapeDtypeStruct((1, 128), f32)],
        scratch_shapes=[pltpu.VMEM((1, D), f32)], compiler_params=_cp("arbitrary"))(y, target)


def pool_fwd(x, w, scale, name, exch=None, tm=256):
    T, D = x.shape
    G, C, _ = w.shape
    tm = _row_tile(T, tm)
    assert all(wd & (wd - 1) == 0 and wd - 1 <= POOL_HALO for wd in POOL_WINDOWS) and tm >= POOL_HALO

    def body(x_ref, w_ref, s_ref, u_ref, p_ref, halo):
        i = pl.program_id(0)

        @pl.when(i == 0)
        def _():
            halo[...] = jnp.zeros_like(halo)

        cur = x_ref[...]
        cat = jnp.concatenate([halo[...], cur], axis=0)
        halo[...] = cur[tm - POOL_HALO:, :]
        t1 = i * tm + lax.broadcasted_iota(jnp.int32, (tm, 1), 0) + 1
        for gi, wd in enumerate(POOL_WINDOWS):
            lo, hi = gi * C, (gi + 1) * C
            win = cat[:, lo:hi]
            sh = 1
            while sh < wd:
                win = win + pltpu.roll(win, sh, 0)
                sh *= 2
            cnt = jnp.minimum(t1, wd).astype(f32)
            pooled = (win[POOL_HALO:, :] / cnt - cur[:, lo:hi]).astype(bf16)
            p_ref[:, lo:hi] = pooled
            u_ref[:, lo:hi] = jnp.dot(pooled, w_ref[gi], preferred_element_type=f32) * s_ref[:, lo:hi]

    row = pl.BlockSpec((tm, D), lambda i: (i, 0))
    return _run(
        body, (x, w, scale.reshape(1, D)), exch, name=name, grid=(T // tm,),
        in_specs=[row, pl.BlockSpec((G, C, C), lambda i: (0, 0, 0)), pl.BlockSpec((1, D), lambda i: (0, 0))],
        out_specs=[row, row], out_shape=[jax.ShapeDtypeStruct((T, D), f32), jax.ShapeDtypeStruct((T, D), bf16)],
        scratch_shapes=[pltpu.VMEM((POOL_HALO, D), f32)], sem=("arbitrary",))


def pool_bwd(du, pooled, w, scale, alpha, name, tm=256):
    T, D = du.shape
    G, C, _ = w.shape
    tm = _row_tile(T, tm)
    nt = T // tm
    n = tm + POOL_HALO

    def body(du_ref, p_ref, w_ref, s_ref, dx_ref, dw_ref, dsc_ref, halo):
        i = pl.program_id(0)

        @pl.when(i == 0)
        def _():
            halo[...] = jnp.zeros_like(halo)
            dw_ref[...] = jnp.zeros_like(dw_ref)
            dsc_ref[...] = jnp.zeros_like(dsc_ref)

        t1 = (nt - 1 - i) * tm + lax.broadcasted_iota(jnp.int32, (tm, 1), 0) + 1
        for gi, wd in enumerate(POOL_WINDOWS):
            lo, hi = gi * C, (gi + 1) * C
            du_g, pb = du_ref[:, lo:hi], p_ref[:, lo:hi]
            yg = jnp.dot(pb, w_ref[gi], preferred_element_type=f32)
            dsc_ref[:, lo:hi] += jnp.sum(du_g * yg, axis=0, keepdims=True)
            dyg = (du_g * s_ref[:, lo:hi]).astype(bf16)
            dw_ref[gi] += lax.dot_general(pb, dyg, TN_DIMS, preferred_element_type=f32)
            dp = lax.dot_general(dyg, w_ref[gi], NT_DIMS, preferred_element_type=f32)
            e = dp / jnp.minimum(t1, wd).astype(f32)
            win = jnp.concatenate([e, halo[:, lo:hi]], axis=0)
            halo[:, lo:hi] = e[:POOL_HALO, :]
            sh = 1
            while sh < wd:
                win = win + pltpu.roll(win, n - sh, 0)
                sh *= 2
            dx_ref[:, lo:hi] = alpha * du_g + win[:tm, :] - dp

    row = pl.BlockSpec((tm, D), lambda i: (nt - 1 - i, 0))
    return _pcall(
        body, name=name, grid=(nt,),
        in_specs=[row, row, pl.BlockSpec((G, C, C), lambda i: (0, 0, 0)), pl.BlockSpec((1, D), lambda i: (0, 0))],
        out_specs=[row, pl.BlockSpec((G, C, C), lambda i: (0, 0, 0)), pl.BlockSpec((1, D), lambda i: (0, 0))],
        out_shape=[jax.ShapeDtypeStruct((T, D), f32), jax.ShapeDtypeStruct((G, C, C), f32), jax.ShapeDtypeStruct((1, D), f32)],
        scratch_shapes=[pltpu.VMEM((POOL_HALO, D), f32)], compiler_params=_cp("arbitrary"))(du, pooled, w, scale.reshape(1, D))


def _sigmoid(x):
    return 0.5 * jnp.tanh(0.5 * x) + 0.5


def _halves(arr):
    J = arr.shape[0] // 2
    view = arr.reshape(2, J, *arr.shape[1:])

    def spec(rows, row_index):
        return pl.BlockSpec((2, None, rows, arr.shape[-1]), lambda j, i: (0, j, row_index(i), 0))

    return view, spec


def ffn_up_gate(x, w, cw, cb, name, exch=None, tm=256):
    T, D = x.shape
    J2, FB, _ = w.shape
    J = J2 // 2
    tm = _row_tile(T, tm)

    def body(x_ref, u_ref, w_ref, b_ref, h_ref, c_ref, a_ref, halo):
        i = pl.program_id(1)

        @pl.when(i == 0)
        def _():
            halo[...] = jnp.zeros_like(halo)

        x_t = x_ref[...]
        conv = []
        for s in range(2):
            cur = lax.dot_general(x_t, u_ref[s], NT_DIMS, preferred_element_type=f32)
            h_ref[s] = cur
            cat = jnp.concatenate([halo[s], cur], axis=0)
            halo[s] = cur[tm - CONV_HALO:, :]
            out = b_ref[s] + w_ref[s, 0:1, :] * pltpu.roll(cat, 2, 0)[CONV_HALO:, :]
            out = out + w_ref[s, 1:2, :] * pltpu.roll(cat, 1, 0)[CONV_HALO:, :]
            conv.append(out + w_ref[s, 2:3, :] * cur)
            c_ref[s] = conv[s]
        a_ref[...] = (conv[0] * _sigmoid(conv[0]) * conv[1]).astype(bf16)

    (u4, u_spec), (w4, w_spec), (b4, b_spec) = _halves(w), _halves(cw), _halves(cb)
    hc_spec = pl.BlockSpec((2, None, tm, FB), lambda j, i: (0, j, i, 0))
    hc_shape = jax.ShapeDtypeStruct((2, J, T, FB), f32)
    res = _run(
        body, (x, u4, w4, b4), exch, name=name, grid=(J, T // tm),
        in_specs=[pl.BlockSpec((tm, D), lambda j, i: (i, 0)), u_spec(FB, lambda i: 0), w_spec(3, lambda i: 0), b_spec(1, lambda i: 0)],
        out_specs=[hc_spec, hc_spec, pl.BlockSpec((None, tm, FB), lambda j, i: (j, i, 0))],
        out_shape=[hc_shape, hc_shape, jax.ShapeDtypeStruct((J, T, FB), bf16)],
        scratch_shapes=[pltpu.VMEM((2, CONV_HALO, FB), f32)], sem=("parallel", "arbitrary"))
    (h, c, a), got = res if exch is not None else (res, None)
    out = (h.reshape(J2, T, FB), c.reshape(J2, T, FB), a)
    return out if exch is None else (out, got)


def ffn_down_gate_bwd(ds, wd, h, c, cw, name, tm=512):
    J2, T, FB = h.shape
    J = J2 // 2
    D = ds.shape[1]
    tm = _row_tile(T, tm)
    nt = T // tm
    n = tm + CONV_HALO

    def body(ds_ref, wd_ref, h_ref, c_ref, w_ref, dh_ref, dw_ref, db_ref, halo):
        i = pl.program_id(1)

        @pl.when(i == 0)
        def _():
            for r in (halo, dw_ref, db_ref):
                r[...] = jnp.zeros_like(r)

        da_t = lax.dot_general(ds_ref[...], wd_ref[...], NT_DIMS, preferred_element_type=f32)
        gate, val = c_ref[0], c_ref[1]
        sg = _sigmoid(gate)
        ds = (da_t * val * (sg * (1.0 + gate * (1.0 - sg))), da_t * (gate * sg))
        for s in range(2):
            d, cur = ds[s], h_ref[s]
            cat = jnp.concatenate([d, halo[s]], axis=0)
            halo[s] = d[:CONV_HALO, :]
            d1 = pltpu.roll(cat, n - 1, 0)[:tm, :]
            d2 = pltpu.roll(cat, n - 2, 0)[:tm, :]
            db_ref[s] += jnp.sum(d, axis=0, keepdims=True)
            dw_ref[s, 0:1, :] += jnp.sum(d2 * cur, axis=0, keepdims=True)
            dw_ref[s, 1:2, :] += jnp.sum(d1 * cur, axis=0, keepdims=True)
            dw_ref[s, 2:3, :] += jnp.sum(d * cur, axis=0, keepdims=True)
            dh_ref[s] = (w_ref[s, 2:3, :] * d + w_ref[s, 1:2, :] * d1 + w_ref[s, 0:1, :] * d2).astype(bf16)

    (h4, h_spec), (c4, _), (w4, w_spec) = _halves(h), _halves(c), _halves(cw)
    rev = lambda i: nt - 1 - i
    b_spec = pl.BlockSpec((2, None, 1, FB), lambda j, i: (0, j, 0, 0))
    dh, dw, db = _pcall(
        body, name=name, grid=(J, nt),
        in_specs=[pl.BlockSpec((tm, D), lambda j, i: (nt - 1 - i, 0)), pl.BlockSpec((None, FB, D), lambda j, i: (j, 0, 0)),
                  h_spec(tm, rev), h_spec(tm, rev), w_spec(3, lambda i: 0)],
        out_specs=[h_spec(tm, rev), w_spec(3, lambda i: 0), b_spec],
        out_shape=[jax.ShapeDtypeStruct((2, J, T, FB), bf16), jax.ShapeDtypeStruct((2, J, 3, FB), f32),
                   jax.ShapeDtypeStruct((2, J, 1, FB), f32)],
        scratch_shapes=[pltpu.VMEM((2, CONV_HALO, FB), f32)], compiler_params=_cp("parallel", "arbitrary"))(ds, wd, h4, c4, w4)
    return dh.reshape(J2, T, FB), dw.reshape(J2, 3, FB), db.reshape(J2, 1, FB)


def _sum_rhs(tk, strict):
    r = lax.broadcasted_iota(jnp.int32, (2 * tk, 2 * tk), 0) % tk
    c = lax.broadcasted_iota(jnp.int32, (2 * tk, 2 * tk), 1)
    return jnp.logical_or(c >= tk, (r > c) if strict else (r >= c)).astype(bf16)


def _split_sums(x, rhs):
    hi = x.astype(bf16)
    lo = (x - hi.astype(f32)).astype(bf16)
    return jnp.dot(jnp.concatenate([hi, lo], axis=1), rhs, preferred_element_type=f32)


def _key_offset(tq, tk):
    return lax.broadcasted_iota(jnp.int32, (tq, tk), 1) - lax.broadcasted_iota(jnp.int32, (tq, tk), 0)


def _attn_geometry(T, tq, tk, nsub):
    tq = _row_tile(T, tq)
    nsub = max(1, min(nsub, T // tq))
    while T % (nsub * tq):
        nsub -= 1
    return tq, min(tk, tq), nsub


def _chain_block(it, nkb, tk):
    return it < nkb, pl.multiple_of(jnp.maximum(nkb - 1 - it, 0) * tk, tk)


def _sweep_live(it, rs, nkb):
    m = None
    for r, n in zip(rs, nkb):
        ra = jnp.where(it < n, r, 2.0 * EXP_UNDERFLOW)
        m = ra if m is None else jnp.maximum(m, ra)
    return jnp.max(m) > EXP_UNDERFLOW


def attn_fwd(qkv, name, exch=None, tq=256, tk=128, nsub=4):
    T, D3 = qkv.shape
    D = D3 // 3
    H = D // HEAD_DIM
    tq, tk, nsub = _attn_geometry(T, tq, tk, nsub)
    bq = nsub * tq
    scale = HEAD_DIM ** -0.5

    never = -(tq + tk)
    R = range(nsub)

    def body(q_ref, k_ref, v_ref, o_ref, of_ref):
        i = pl.program_id(1)
        rhs = _sum_rhs(tk, False)
        cmr = _key_offset(tq, tk)
        nkb = [(i * nsub + a + 1) * (tq // tk) for a in R]
        qs = [q_ref[a * tq:(a + 1) * tq, :] for a in R]

        def cond(c):
            return _sweep_live(c[0], c[1], nkb)

        def step(c):
            it, rs, accs, fines = c
            vk = [_chain_block(it, nkb[a], tk) for a in R]
            k0 = [vk[a][1] for a in R]
            off = (it + 1) * tk - tq
            zs = [lax.dot_general(qs[a], k_ref[pl.ds(k0[a], tk), :], NT_DIMS, preferred_element_type=f32) * scale for a in R]
            masks = [cmr < jnp.where(vk[a][0], off, never) for a in R]
            lsms = [jnp.where(masks[a], jnp.minimum(-zs[a], 0.0) - jnp.log(1.0 + jnp.exp(-jnp.abs(zs[a]))), 0.0) for a in R]
            lsum = [_split_sums(lsms[a], rhs) for a in R]
            ws = [jnp.where(masks[a], jnp.exp(zs[a] + lsum[a][:, :tk] + rs[a]), 0.0) for a in R]
            his = [ws[a].astype(bf16) for a in R]
            los = [(ws[a] - his[a].astype(f32)).astype(bf16) for a in R]
            accs2 = [accs[a] + jnp.dot(his[a], v_ref[pl.ds(k0[a], tk), :], preferred_element_type=f32) for a in R]
            fines2 = [fines[a] + jnp.dot(los[a], v_ref[pl.ds(k0[a], tk), :], preferred_element_type=f32) for a in R]
            return it + 1, [rs[a] + lsum[a][:, tk:] for a in R], accs2, fines2

        zeros = [jnp.zeros((tq, HEAD_DIM), f32)] * nsub
        _, _, accs, fines = lax.while_loop(cond, step, (jnp.int32(0), [jnp.zeros((tq, tk), f32)] * nsub, zeros, zeros))
        for a in R:
            o_ref[a * tq:(a + 1) * tq, :] = accs[a].astype(bf16)
            of_ref[a * tq:(a + 1) * tq, :] = accs[a] + fines[a]

    qblk = pl.BlockSpec((bq, HEAD_DIM), lambda h, i: (i, h))
    return _run(
        body, (qkv, qkv, qkv), exch, name=name, grid=(H, T // bq),
        in_specs=[qblk, pl.BlockSpec((T, HEAD_DIM), lambda h, i: (0, H + h)),
                  pl.BlockSpec((T, HEAD_DIM), lambda h, i: (0, 2 * H + h))],
        out_specs=[qblk, qblk], out_shape=[jax.ShapeDtypeStruct((T, D), bf16), jax.ShapeDtypeStruct((T, D), f32)],
        sem=("parallel", "arbitrary"))


def attn_bwd(qkv, do, o_fine, name, exch=None, tq=256, tk=128, nsub=4):
    T, D3 = qkv.shape
    D = D3 // 3
    H = D // HEAD_DIM
    tq, tk, nsub = _attn_geometry(T, tq, tk, nsub)
    assert tk == HEAD_DIM
    bq = nsub * tq
    nq = T // bq
    scale = HEAD_DIM ** -0.5
    never = -(tq + tk)
    R = range(nsub)

    def body(q_ref, k_ref, v_ref, do_ref, of_ref, dq_ref, dk_ref, dv_ref, dk_acc, dv_acc):
        i = pl.program_id(1)

        @pl.when(i == 0)
        def _():
            dk_acc[...] = jnp.zeros_like(dk_acc)
            dv_acc[...] = jnp.zeros_like(dv_acc)

        rhs_incl, rhs_excl = _sum_rhs(tk, False), _sum_rhs(tk, True)
        cmr = _key_offset(tq, tk)
        nkb = [(i * nsub + a + 1) * (tq // tk) for a in R]
        qs = [q_ref[a * tq:(a + 1) * tq, :] for a in R]
        dos = [do_ref[a * tq:(a + 1) * tq, :] for a in R]
        zeros = [jnp.zeros((tq, tk), f32)] * nsub

        def cond(c):
            return _sweep_live(c[0], c[1], nkb)

        def weights(it, rs):
            vk = [_chain_block(it, nkb[a], tk) for a in R]
            k0 = [vk[a][1] for a in R]
            off = (it + 1) * tk - tq
            zs = [lax.dot_general(qs[a], k_ref[pl.ds(k0[a], tk), :], NT_DIMS, preferred_element_type=f32) * scale for a in R]
            das = [lax.dot_general(dos[a], v_ref[pl.ds(k0[a], tk), :], NT_DIMS, preferred_element_type=f32) for a in R]
            masks = [cmr < jnp.where(vk[a][0], off, never) for a in R]
            es = [jnp.exp(-jnp.abs(zs[a])) for a in R]
            lsms = [jnp.where(masks[a], jnp.minimum(-zs[a], 0.0) - jnp.log(1.0 + es[a]), 0.0) for a in R]
            lsum = [_split_sums(lsms[a], rhs_incl) for a in R]
            ws = [jnp.where(masks[a], jnp.exp(zs[a] + lsum[a][:, :tk] + rs[a]), 0.0) for a in R]
            gs = [ws[a] * das[a] for a in R]
            gsum = [_split_sums(gs[a], rhs_excl) for a in R]
            return k0, masks, zs, es, ws, gs, lsum, gsum

        g_total = [_split_sums(dos[a].astype(f32) * of_ref[a * tq:(a + 1) * tq, :], rhs_incl)[:, tk:] for a in R]

        def grad_step(c):
            it, rs, runs, dqs = c
            k0, masks, zs, es, ws, gs, lsum, gsum = weights(it, rs)
            sig = [_sigmoid(zs[a]) for a in R]
            prefix = [g_total[a] - (runs[a] + gsum[a][:, :tk]) for a in R]
            dzs = [(jnp.where(masks[a], gs[a] - sig[a] * prefix[a], 0.0) * scale).astype(bf16) for a in R]
            wbs = [ws[a].astype(bf16) for a in R]
            dqs2 = [dqs[a] + jnp.dot(dzs[a], k_ref[pl.ds(k0[a], tk), :], preferred_element_type=f32) for a in R]
            dks = [lax.dot_general(dzs[a], qs[a], TN_DIMS, preferred_element_type=f32) for a in R]
            dvs = [lax.dot_general(wbs[a], dos[a], TN_DIMS, preferred_element_type=f32) for a in R]
            for a in R:
                dk_acc[pl.ds(k0[a], tk), :] += dks[a]
                dv_acc[pl.ds(k0[a], tk), :] += dvs[a]
            return it + 1, [rs[a] + lsum[a][:, tk:] for a in R], [runs[a] + gsum[a][:, tk:] for a in R], dqs2

        init = (jnp.int32(0), zeros, zeros, [jnp.zeros((tq, HEAD_DIM), f32)] * nsub)
        _, _, _, dqs = lax.while_loop(cond, grad_step, init)
        for a in R:
            dq_ref[a * tq:(a + 1) * tq, :] = dqs[a].astype(bf16)

        @pl.when(i == nq - 1)
        def _():
            dk_ref[...] = dk_acc[...].astype(bf16)
            dv_ref[...] = dv_acc[...].astype(bf16)

    qblk = pl.BlockSpec((bq, HEAD_DIM), lambda h, i: (i, h))
    head = pl.BlockSpec((T, HEAD_DIM), lambda h, i: (0, h))
    return _run(
        body, (qkv, qkv, qkv, do, o_fine), exch, name=name, grid=(H, nq),
        in_specs=[qblk, pl.BlockSpec((T, HEAD_DIM), lambda h, i: (0, H + h)),
                  pl.BlockSpec((T, HEAD_DIM), lambda h, i: (0, 2 * H + h)), qblk, qblk],
        out_specs=[qblk, head, head], out_shape=[jax.ShapeDtypeStruct((T, D), bf16)] * 3,
        scratch_shapes=[pltpu.VMEM((T, HEAD_DIM), f32), pltpu.VMEM((T, HEAD_DIM), f32)],
        sem=("parallel", "arbitrary"))


def sum_partials(p, name, tr=256):
    K, R, C = p.shape
    tr = _row_tile(R, tr)

    def body(p_ref, o_ref):
        g = p_ref[0].astype(f32)
        for k in range(1, K):
            g = g + p_ref[k].astype(f32)
        o_ref[...] = g

    return _pcall(
        body, name=name, grid=(R // tr,), in_specs=[pl.BlockSpec((K, tr, C), lambda i: (0, i, 0))],
        out_specs=pl.BlockSpec((tr, C), lambda i: (i, 0)), out_shape=jax.ShapeDtypeStruct((R, C), f32),
        compiler_params=_cp("parallel"))(p)


def adamw(p, w, m, v, slot, outs, name):
    K, R, C = p.shape
    L = w.shape[0]
    tc = _lane_tile(C, 1024)
    row_bytes = 2 * tc * (K * p.dtype.itemsize + 7 * 4)
    tr = _row_tile(R, max(8, MM_VMEM_BUDGET_BYTES // row_bytes // 8 * 8))
    c1 = 1.0 - ADAM_B1 ** ADAM_STEP
    c2 = 1.0 - ADAM_B2 ** ADAM_STEP
    n_old = 0 if outs is None else 4

    def body(p_ref, w_ref, m_ref, v_ref, *refs):
        g_ref, d_ref, nm_ref, nv_ref = refs[n_old:]
        g = p_ref[0].astype(f32)
        for k in range(1, K):
            g = g + p_ref[k].astype(f32)
        nm = ADAM_B1 * m_ref[...] + (1.0 - ADAM_B1) * g
        nv = ADAM_B2 * v_ref[...] + (1.0 - ADAM_B2) * (g * g)
        g_ref[...] = g
        nm_ref[...] = nm
        nv_ref[...] = nv
        d_ref[...] = -ADAM_LR * ((nm / c1) / (jnp.sqrt(nv / c2) + ADAM_EPS) + ADAM_WD * w_ref[...])

    blk = pl.BlockSpec((None, tr, tc), lambda i, j: (slot, i, j))
    old = [] if outs is None else list(outs)
    return _pcall(
        body, name=name, grid=(R // tr, C // tc),
        in_specs=[pl.BlockSpec((K, tr, tc), lambda i, j: (0, i, j)), blk, blk, blk] + [pl.BlockSpec(memory_space=pl.ANY)] * n_old,
        out_specs=[blk] * 4, out_shape=[jax.ShapeDtypeStruct((L, R, C), f32)] * 4,
        input_output_aliases={4 + t: t for t in range(n_old)}, compiler_params=_cp("parallel", "parallel"))(p, w, m, v, *old)


def _pack(parts):
    flat, slices, off = [], [], 0
    for a in parts:
        nel = a.size
        pad = -nel % PACK_ALIGN
        flat.append(jnp.pad(a.reshape(-1).astype(f32), (0, pad)))
        slices.append((off, nel, a.shape))
        off += nel + pad
    return jnp.concatenate(flat).reshape(-1, 128), slices


def _unpack(packed, slices):
    flat = packed.reshape(-1)
    return [flat[off:off + nel].reshape(shape) for off, nel, shape in slices]


def kernel(x, pool_w, pool_scale, attn_w_qkv, attn_w_o, ffn_w_up, ffn_conv_w, ffn_conv_b, ffn_w_down, ln_mix_g, ln_mix_b, ln_ffn_g, ln_ffn_b, loss_target, m_pool_w, m_pool_scale, m_attn_w_qkv, m_attn_w_o, m_ffn_w_up, m_ffn_conv_w, m_ffn_conv_b, m_ffn_w_down, m_ln_mix_g, m_ln_mix_b, m_ln_ffn_g, m_ln_ffn_b, v_pool_w, v_pool_scale, v_attn_w_qkv, v_attn_w_o, v_ffn_w_up, v_ffn_conv_w, v_ffn_conv_b, v_ffn_w_down, v_ln_mix_g, v_ln_mix_b, v_ln_ffn_g, v_ln_ffn_b):
    _, T, D = x.shape
    depth = ln_mix_g.shape[0]
    alpha = (2.0 * depth) ** 0.25
    G, CS, C = pool_w.shape[1:]
    FB = ffn_w_up.shape[2]
    JH = N_DEV // 2
    me = 4 * lax.axis_index("x") + 2 * lax.axis_index("y") + lax.axis_index("c")

    up_t, m_up_t, v_up_t = (jnp.swapaxes(a, 1, 2) for a in (ffn_w_up, m_ffn_w_up, v_ffn_w_up))
    shards = {}
    for l in range(depth):
        j = l // 2
        shards["up", l], shards["down", l] = up_t[l].astype(bf16), ffn_w_down[l].astype(bf16)
        if l % 2 == 0:
            shards["pool", l] = pool_w[j].astype(bf16)
        else:
            shards["qkv", l], shards["wo", l] = attn_w_qkv[j].astype(bf16), attn_w_o[j].astype(bf16)
    full = {}

    def gather_in(call, keys):
        keys = [k for k in keys if k[1] < depth]
        if not keys:
            return call(None)
        res, got = call(([shards[k] for k in keys], False))
        full.update(zip(keys, got))
        return res

    full["pool", 0], cw_all = all_gather([shards["pool", 0], ffn_conv_w], "gather_first")

    xf = x[0]
    xb = xf.astype(bf16)
    saved, weights = [], []
    for l in range(depth):
        j, sv, wl = l // 2, {}, {}
        if l % 2 == 0:
            wl["pool"] = full["pool", l].transpose(1, 0, 2, 3).reshape(G, C, C)
            u, sv["pooled"] = gather_in(lambda e: pool_fwd(xf, wl["pool"], pool_scale[j], f"pool_fwd_l{l}", exch=e),
                                        [("up", 0)] if l == 0 else [])
            up_rides = ([("down", l)] if l == 0 else [("qkv", l + 1)]) + [("wo", l + 1)]
            down_rides = [("qkv", l + 1)] if l == 0 else []
        else:
            wl["qkv"], wl["wo"] = full["qkv", l], full["wo", l].reshape(1, D, D)
            sv["xin_b"] = xb
            sv["qkv"] = mm_nn(xb, wl["qkv"], bf16, f"qkv_l{l}", out_cols=True)
            sv["o"], sv["o_fine"] = gather_in(lambda e: attn_fwd(sv["qkv"], f"attn_fwd_l{l}", exch=e),
                                [("up", l), ("down", l), ("pool", l + 1)])
            u = mm_nn(sv["o"], wl["wo"], f32, f"attn_out_l{l}")[0]
            up_rides, down_rides = [("up", l + 1)], [("down", l + 1)]
        x1, sv["x1_b"], sv["xhat1"], sv["rstd1"] = ln_fwd(xf, u, ln_mix_g[l], ln_mix_b[l], alpha, f"ln_mix_l{l}")
        wl["up"], wl["cw"], wl["cb"] = full["up", l], cw_all[:, l], ffn_conv_b[l].reshape(N_DEV, 1, FB)
        sv["h"], sv["c"], sv["a"] = gather_in(
            lambda e: ffn_up_gate(sv["x1_b"], wl["up"], wl["cw"], wl["cb"], f"ffn_up_l{l}", exch=e), up_rides)
        wl["down"] = full["down", l].reshape(JH, FB, D)
        f = gather_in(lambda e: mm_nn_sum(sv["a"], wl["down"], f"ffn_down_l{l}", exch=e), down_rides)
        xf, xb, sv["xhat2"], sv["rstd2"] = ln_fwd(x1, f, ln_ffn_g[l], ln_ffn_b[l], alpha, f"ln_ffn_l{l}")
        saved.append(sv)
        weights.append(wl)

    dx, loss_row = loss_head(xf, loss_target[0], "loss_head")

    landed = {}

    def scatter_in(call, parts):
        res, got = call(([p for _, p in parts], True))
        landed.update(zip([k for k, _ in parts], got))
        return res

    grads = [None] * depth
    res, pending = None, []
    for l in reversed(range(depth)):
        wl, sv, j, gl = weights[l], saved[l], l // 2, {}
        ds2, ds2_b, gl["ln_ffn_g"], gl["ln_ffn_b"] = ln_bwd(dx, res, alpha, sv["xhat2"], sv["rstd2"], ln_ffn_g[l], f"ln_ffn_bwd_l{l}")
        g_down = mm_tn(sv["a"], ds2_b[None], bf16, f"ffn_down_dw_l{l}").reshape(N_DEV, FB // 2, D)
        dh, gl["cw"], gl["cb"] = ffn_down_gate_bwd(ds2_b, wl["down"], sv["h"], sv["c"], wl["cw"], f"ffn_down_dx_l{l}")
        g_up = scatter_in(lambda e: mm_tn(dh, sv["x1_b"][None], bf16, f"ffn_up_dw_l{l}", exch=e),
                          [(("down", l), g_down)] + pending)
        dx1 = scatter_in(lambda e: mm_nn_sum(dh, wl["up"], f"ffn_up_dx_l{l}", exch=e), [(("up", l), g_up)])
        ds1, ds1_b, gl["ln_mix_g"], gl["ln_mix_b"] = ln_bwd(dx1, ds2, alpha, sv["xhat1"], sv["rstd1"], ln_mix_g[l], f"ln_mix_bwd_l{l}")
        if l % 2 == 0:
            dx, g_pool, gl["pool_scale"] = pool_bwd(ds1, sv["pooled"], wl["pool"], pool_scale[j], alpha, f"pool_bwd_l{l}")
            res, pending = None, [(("pool", l), g_pool.reshape(G, N_DEV, CS, C).transpose(1, 0, 2, 3).astype(bf16))]
        else:
            do = mm_nt(ds1_b, wl["wo"], bf16, f"attn_out_dx_l{l}")[0]
            g_wo = mm_tn(sv["o"][None], ds1_b[None], bf16, f"attn_out_dw_l{l}").reshape(N_DEV, D // N_DEV, D)
            dqkv = jnp.concatenate(scatter_in(lambda e: attn_bwd(sv["qkv"], do, sv["o_fine"], f"attn_bwd_l{l}", exch=e),
                                              [(("wo", l), g_wo)]), axis=1)
            g_qkv = mm_tn(sv["xin_b"][None], dqkv, bf16, f"qkv_dw_l{l}", b_cols=N_DEV)
            dx = scatter_in(lambda e: mm_nt_sum(dqkv, wl["qkv"], f"qkv_dx_l{l}", a_cols=True, exch=e), [(("qkv", l), g_qkv)])
            res, pending = ds1, []
        grads[l] = gl
    assert res is None, "the first layer is a pooling layer: its backward adds the residual path itself"
    grad_x = dx[None]
    if pending:
        landed.update(zip([k for k, _ in pending], scatter_partials([p for _, p in pending], "scatter_last")))

    def stack_update(name, kind, w, m, v):
        shape = w.shape
        rows = w[0].size // shape[-1]
        flat = [a.reshape(shape[0], rows, shape[-1]) for a in (w, m, v)]
        outs = None
        for i, l in enumerate(sorted(l for k, l in landed if k == kind)):
            outs = adamw(landed[kind, l].reshape(N_DEV, rows, shape[-1]), *flat, i, outs, f"adamw_{name}_{i}")
        return [o.reshape(shape) for o in outs]

    big = {
        "pool_w": stack_update("pool_w", "pool", pool_w, m_pool_w, v_pool_w),
        "attn_w_qkv": stack_update("attn_w_qkv", "qkv", attn_w_qkv, m_attn_w_qkv, v_attn_w_qkv),
        "attn_w_o": stack_update("attn_w_o", "wo", attn_w_o, m_attn_w_o, v_attn_w_o),
        "ffn_w_up": [jnp.swapaxes(o, 1, 2) for o in stack_update("ffn_w_up", "up", up_t, m_up_t, v_up_t)],
        "ffn_w_down": stack_update("ffn_w_down", "down", ffn_w_down, m_ffn_w_down, v_ffn_w_down),
    }

    def per_layer(key):
        return jnp.stack([grads[l][key].reshape(-1) for l in range(depth)])

    small_parts = [per_layer("ln_mix_g"), per_layer("ln_mix_b"), per_layer("ln_ffn_g"), per_layer("ln_ffn_b"),
                   jnp.stack([grads[l]["pool_scale"].reshape(-1) for l in range(0, depth, 2)]),
                   per_layer("cb"), jnp.stack([grads[l]["cw"] for l in range(depth)]), loss_row[0, :1]]
    packed, slices = _pack(small_parts)
    total = sum_partials(all_gather([packed], "gather_small")[0], "sum_small")
    g_mix_g, g_mix_b, g_ffn_g, g_ffn_b, g_scale, g_cb, g_cw_all, loss = _unpack(total, slices)
    g_cw = lax.dynamic_index_in_dim(g_cw_all, me, axis=1, keepdims=False)

    small_w = [ln_mix_g, ln_mix_b, ln_ffn_g, ln_ffn_b, pool_scale, ffn_conv_b, ffn_conv_w]
    small_m = [m_ln_mix_g, m_ln_mix_b, m_ln_ffn_g, m_ln_ffn_b, m_pool_scale, m_ffn_conv_b, m_ffn_conv_w]
    small_v = [v_ln_mix_g, v_ln_mix_b, v_ln_ffn_g, v_ln_ffn_b, v_pool_scale, v_ffn_conv_b, v_ffn_conv_w]
    small_g = [g_mix_g, g_mix_b, g_ffn_g, g_ffn_b, g_scale, g_cb.reshape(ffn_conv_b.shape), g_cw]
    pw, wslices = _pack(small_w)
    res = adamw(_pack(small_g)[0][None], pw[None], _pack(small_m)[0][None], _pack(small_v)[0][None], 0, None, "adamw_small")
    small = dict(zip(["ln_mix_g", "ln_mix_b", "ln_ffn_g", "ln_ffn_b", "pool_scale", "ffn_conv_b", "ffn_conv_w"],
                     zip(*[_unpack(r[0], wslices) for r in res])))

    order = ["pool_w", "pool_scale", "attn_w_qkv", "attn_w_o", "ffn_w_up", "ffn_conv_w", "ffn_conv_b", "ffn_w_down",
             "ln_mix_g", "ln_mix_b", "ln_ffn_g", "ln_ffn_b"]
    table = {**big, **{k: list(val) for k, val in small.items()}}
    outs = [loss.reshape(()), grad_x]
    for t in range(4):
        outs += [table[name][t] for name in order]
    return tuple(outs)
```
